```python
import math
import jax, jax.numpy as jnp
from jax import lax
import numpy as np


D_MODEL = 2048
BATCH = 2
SEQ = 4096
DEPTH = 2

MLA_HEADS = 8
MLA_Q_RANK = 512
MLA_KV_RANK = 256
MLA_NOPE_DIM = 128
MLA_ROPE_DIM = 64
MLA_V_DIM = 128
ROPE_THETA = 10000.0
GMLP_GROUPS = 4
GMLP_GROUP_DIM = 128
GMLP_WIDTH = GMLP_GROUPS * GMLP_GROUP_DIM
CHUNK = 128
DIFF_HEADS = 4
DIFF_QK_DIM = 64
DIFF_V_DIM = 128
REL_BUCKETS = 32
REL_MAX_EXACT = 16
REL_MAX_DIST = 128
Q_BLOCK = 128
MIX_WIDTH = MLA_HEADS * MLA_V_DIM + GMLP_WIDTH + DIFF_HEADS * DIFF_V_DIM
IN_SPLITS = (MLA_Q_RANK, MLA_KV_RANK, MLA_ROPE_DIM, GMLP_WIDTH, GMLP_WIDTH,
             DIFF_HEADS * 2 * DIFF_QK_DIM, DIFF_HEADS * 2 * DIFF_QK_DIM, DIFF_HEADS * DIFF_V_DIM)
IN_WIDTH = sum(IN_SPLITS)
IN_SPLIT_POINTS = tuple(int(v) for v in np.cumsum(IN_SPLITS)[:-1])
N_EXPERTS = 64
N_EXPERT_GROUPS = 8
TOPK_GROUPS = 4
TOP_K = 8
EXPERT_DIM = 512
SHARED_DIM = 512
ROUTED_SCALE = 2.5
MOE_BLOCK = 128
DEEPNORM_ALPHA = (2 * DEPTH) ** 0.25
DEEPNORM_BETA = (8 * DEPTH) ** -0.25

kernel_name = 'hybrid_mla_gmlp_diffattn_moe_deepnorm'


def layer_norm(x, g, b, eps=1e-5):
    xf = x.astype(jnp.float32)
    mu = jnp.mean(xf, axis=-1, keepdims=True)
    var = jnp.mean(jnp.square(xf - mu), axis=-1, keepdims=True)
    return ((xf - mu) * lax.rsqrt(var + eps) * g.astype(jnp.float32) + b.astype(jnp.float32)).astype(x.dtype)


def rms_norm(x, g, eps=1e-6):
    xf = x.astype(jnp.float32)
    ms = jnp.mean(jnp.square(xf), axis=-1, keepdims=True)
    return (xf * lax.rsqrt(ms + eps) * g.astype(jnp.float32)).astype(x.dtype)


def rope(x, pos):
    half = x.shape[-1] // 2
    inv = ROPE_THETA ** (-jnp.arange(half, dtype=jnp.float32) / half)
    ang = pos.astype(jnp.float32)[:, None] * inv[None, :]
    cos = jnp.cos(ang)[None, :, None, :]
    sin = jnp.sin(ang)[None, :, None, :]
    x1 = x[..., :half].astype(jnp.float32)
    x2 = x[..., half:].astype(jnp.float32)
    return jnp.concatenate([x1 * cos - x2 * sin, x2 * cos + x1 * sin], axis=-1).astype(x.dtype)


def t5_causal_bucket(n):
    is_small = n < REL_MAX_EXACT
    nf = jnp.maximum(n, 1).astype(jnp.float32)
    large = REL_MAX_EXACT + (jnp.log(nf / REL_MAX_EXACT) / math.log(REL_MAX_DIST / REL_MAX_EXACT)
                             * (REL_BUCKETS - REL_MAX_EXACT)).astype(jnp.int32)
    large = jnp.minimum(large, REL_BUCKETS - 1)
    return jnp.where(is_small, n, large)


def mla_mixer(c_q, c_kv, k_rope_raw, q_norm_g, w_uq, kv_norm_g, w_ukv, pos):
    B, S, _ = c_q.shape
    q = (rms_norm(c_q, q_norm_g) @ w_uq).reshape(B, S, MLA_HEADS, MLA_NOPE_DIM + MLA_ROPE_DIM)
    q_nope = q[..., :MLA_NOPE_DIM]
    q_rope = rope(q[..., MLA_NOPE_DIM:], pos)
    kv = (rms_norm(c_kv, kv_norm_g) @ w_ukv).reshape(B, S, MLA_HEADS, MLA_NOPE_DIM + MLA_V_DIM)
    k_nope = kv[..., :MLA_NOPE_DIM]
    v = kv[..., MLA_NOPE_DIM:]
    k_rope = rope(k_rope_raw[:, :, None, :], pos)[:, :, 0, :]
    scale = (MLA_NOPE_DIM + MLA_ROPE_DIM) ** -0.5
    k_pos = jnp.arange(S, dtype=jnp.int32)

    def block(i):
        start = i * Q_BLOCK
        qn = lax.dynamic_slice_in_dim(q_nope, start, Q_BLOCK, axis=1)
        qr = lax.dynamic_slice_in_dim(q_rope, start, Q_BLOCK, axis=1)
        s = jnp.einsum('bqhd,bkhd->bhqk', qn, k_nope) + jnp.einsum('bqhr,bkr->bhqk', qr, k_rope)
        s = s.astype(jnp.float32) * scale
        q_pos = start + jnp.arange(Q_BLOCK, dtype=jnp.int32)
        causal = k_pos[None, :] <= q_pos[:, None]
        p = jax.nn.softmax(jnp.where(causal, s, -jnp.inf), axis=-1).astype(v.dtype)
        return jnp.einsum('bhqk,bkhd->bqhd', p, v)

    o = lax.map(block, jnp.arange(S // Q_BLOCK, dtype=jnp.int32))
    return o.transpose(1, 0, 2, 3, 4).reshape(B, S, MLA_HEADS * MLA_V_DIM)


def gmlp_mixer(u, v, ln_g, ln_b, w_s, b_s):
    B, S, _ = u.shape
    u = jax.nn.gelu(u)
    v = layer_norm(jax.nn.gelu(v), ln_g, ln_b)
    vc = v.reshape(B, S // CHUNK, CHUNK, GMLP_GROUPS, GMLP_GROUP_DIM)
    causal = jnp.tril(jnp.ones((CHUNK, CHUNK), dtype=bool))
    ws = jnp.where(causal[None], w_s, 0.0).astype(v.dtype)
    mixed = jnp.einsum('gts,bcsgd->bctgd', ws, vc) + b_s.T[None, None, :, :, None]
    return u * mixed.reshape(B, S, GMLP_WIDTH)


def diff_mixer(q, k, v, lq1, lk1, lq2, lk2, subln_g, rel_table, lambda_init):
    B, S, _ = q.shape
    q = q.reshape(B, S, DIFF_HEADS, 2, DIFF_QK_DIM)
    k = k.reshape(B, S, DIFF_HEADS, 2, DIFF_QK_DIM)
    v = v.reshape(B, S, DIFF_HEADS, DIFF_V_DIM)
    lam = (jnp.exp(jnp.sum(lq1.astype(jnp.float32) * lk1.astype(jnp.float32)))
           - jnp.exp(jnp.sum(lq2.astype(jnp.float32) * lk2.astype(jnp.float32))) + lambda_init)
    scale = DIFF_QK_DIM ** -0.5
    k_pos = jnp.arange(S, dtype=jnp.int32)

    def block(i):
        start = i * Q_BLOCK
        qb = lax.dynamic_slice_in_dim(q, start, Q_BLOCK, axis=1)
        s = jnp.einsum('bqhmd,bkhmd->bhmqk', qb, k).astype(jnp.float32) * scale
        q_pos = start + jnp.arange(Q_BLOCK, dtype=jnp.int32)
        rel = q_pos[:, None] - k_pos[None, :]
        bias = rel_table.astype(jnp.float32)[t5_causal_bucket(jnp.maximum(rel, 0))]
        s = s + bias.transpose(2, 0, 1)[None, :, None, :, :]
        s = jnp.where((rel >= 0)[None, None, None], s, -jnp.inf)
        p = jax.nn.softmax(s, axis=-1)
        a = (p[:, :, 0] - lam * p[:, :, 1]).astype(v.dtype)
        return jnp.einsum('bhqk,bkhd->bqhd', a, v)

    o = lax.map(block, jnp.arange(S // Q_BLOCK, dtype=jnp.int32))
    o = o.transpose(1, 0, 2, 3, 4).reshape(B, S, DIFF_HEADS, DIFF_V_DIM)
    o = rms_norm(o, subln_g) * (1.0 - lambda_init)
    return o.reshape(B, S, DIFF_HEADS * DIFF_V_DIM)


def moe(x, router_w, router_bias, e_gate, e_up, e_down, s_gate, s_up, s_down):
    B, S, D = x.shape
    t = x.reshape(B * S, D)
    T = t.shape[0]
    scores = jax.nn.sigmoid((t @ router_w).astype(jnp.float32))
    biased = scores + router_bias.astype(jnp.float32)
    grouped = biased.reshape(T, N_EXPERT_GROUPS, N_EXPERTS // N_EXPERT_GROUPS)
    group_score = jnp.sum(lax.top_k(grouped, 2)[0], axis=-1)
    _, gidx = lax.top_k(group_score, TOPK_GROUPS)
    gmask = jnp.sum(jax.nn.one_hot(gidx, N_EXPERT_GROUPS, dtype=jnp.float32), axis=1) > 0
    emask = jnp.repeat(gmask, N_EXPERTS // N_EXPERT_GROUPS, axis=1)
    _, eidx = lax.top_k(jnp.where(emask, biased, -jnp.inf), TOP_K)
    w = jnp.take_along_axis(scores, eidx, axis=1)
    w = w / jnp.sum(w, axis=-1, keepdims=True) * ROUTED_SCALE
    gates = jnp.sum(jax.nn.one_hot(eidx, N_EXPERTS, dtype=jnp.float32) * w[..., None], axis=1).astype(x.dtype)

    def block(args):
        tb, gb = args
        g = jnp.einsum('td,edf->tef', tb, e_gate)
        u = jnp.einsum('td,edf->tef', tb, e_up)
        h = jax.nn.silu(g) * u * gb[:, :, None]
        return jnp.einsum('tef,efd->td', h, e_down)

    routed = lax.map(block, (t.reshape(-1, MOE_BLOCK, D), gates.reshape(-1, MOE_BLOCK, N_EXPERTS)))
    shared = (jax.nn.silu(t @ s_gate) * (t @ s_up)) @ s_down
    return (routed.reshape(T, D) + shared).reshape(B, S, D)


def setup_inputs(seed: int = 0) -> dict:
    key = jax.random.key(seed)
    ks = jax.random.split(key, 32)
    L = DEPTH
    D = D_MODEL

    def n(k, shape, scale):
        return scale * jax.random.normal(k, shape, jnp.float32)

    return {
        'x': n(ks[0], (BATCH, SEQ, D), 1.0),
        'rel_bias': n(ks[1], (REL_BUCKETS, DIFF_HEADS), 0.5),
        'w_in': n(ks[2], (L, D, IN_WIDTH), D ** -0.5),
        'mla_q_norm': 1.0 + n(ks[3], (L, MLA_Q_RANK), 0.01),
        'mla_w_uq': n(ks[4], (L, MLA_Q_RANK, MLA_HEADS * (MLA_NOPE_DIM + MLA_ROPE_DIM)), MLA_Q_RANK ** -0.5),
        'mla_kv_norm': 1.0 + n(ks[5], (L, MLA_KV_RANK), 0.01),
        'mla_w_ukv': n(ks[6], (L, MLA_KV_RANK, MLA_HEADS * (MLA_NOPE_DIM + MLA_V_DIM)), MLA_KV_RANK ** -0.5),
        'gmlp_ln_g': 1.0 + n(ks[7], (L, GMLP_WIDTH), 0.01),
        'gmlp_ln_b': n(ks[8], (L, GMLP_WIDTH), 0.01),
        'gmlp_w_s': n(ks[9], (L, GMLP_GROUPS, CHUNK, CHUNK), CHUNK ** -0.5),
        'gmlp_b_s': 1.0 + n(ks[10], (L, GMLP_GROUPS, CHUNK), 0.01),
        'diff_lq1': n(ks[11], (L, DIFF_QK_DIM), 0.1),
        'diff_lk1': n(ks[12], (L, DIFF_QK_DIM), 0.1),
        'diff_lq2': n(ks[13], (L, DIFF_QK_DIM), 0.1),
        'diff_lk2': n(ks[14], (L, DIFF_QK_DIM), 0.1),
        'diff_subln': 1.0 + n(ks[15], (L, DIFF_V_DIM), 0.01),
        'w_o': n(ks[16], (L, MIX_WIDTH, D), MIX_WIDTH ** -0.5 * DEEPNORM_BETA),
        'ln1_g': 1.0 + n(ks[17], (L, D), 0.01),
        'ln1_b': n(ks[18], (L, D), 0.01),
        'router_w': n(ks[19], (L, D, N_EXPERTS), D ** -0.5),
        'router_bias': n(ks[20], (L, N_EXPERTS), 0.01),
        'exp_w_gate': n(ks[21], (L, N_EXPERTS, D, EXPERT_DIM), D ** -0.5),
        'exp_w_up': n(ks[22], (L, N_EXPERTS, D, EXPERT_DIM), D ** -0.5),
        'exp_w_down': n(ks[23], (L, N_EXPERTS, EXPERT_DIM, D), EXPERT_DIM ** -0.5 * DEEPNORM_BETA),
        'shared_w_gate': n(ks[24], (L, D, SHARED_DIM), D ** -0.5),
        'shared_w_up': n(ks[25], (L, D, SHARED_DIM), D ** -0.5),
        'shared_w_down': n(ks[26], (L, SHARED_DIM, D), SHARED_DIM ** -0.5 * DEEPNORM_BETA),
        'ln2_g': 1.0 + n(ks[27], (L, D), 0.01),
        'ln2_b': n(ks[28], (L, D), 0.01),
    }


def reference(x, rel_bias, w_in, mla_q_norm, mla_w_uq, mla_kv_norm, mla_w_ukv,
              gmlp_ln_g, gmlp_ln_b, gmlp_w_s, gmlp_b_s,
              diff_lq1, diff_lk1, diff_lq2, diff_lk2, diff_subln,
              w_o, ln1_g, ln1_b, router_w, router_bias,
              exp_w_gate, exp_w_up, exp_w_down,
              shared_w_gate, shared_w_up, shared_w_down, ln2_g, ln2_b):
    S = x.shape[1]
    pos = jnp.arange(S, dtype=jnp.int32)
    for l in range(DEPTH):
        lambda_init = 0.8 - 0.6 * math.exp(-0.3 * l)
        h = x @ w_in[l]
        c_q, c_kv, k_r, g_u, g_v, d_q, d_k, d_v = jnp.split(h, IN_SPLIT_POINTS, axis=-1)
        o_mla = mla_mixer(c_q, c_kv, k_r, mla_q_norm[l], mla_w_uq[l], mla_kv_norm[l], mla_w_ukv[l], pos)
        o_gmlp = gmlp_mixer(g_u, g_v, gmlp_ln_g[l], gmlp_ln_b[l], gmlp_w_s[l], gmlp_b_s[l])
        o_diff = diff_mixer(d_q, d_k, d_v, diff_lq1[l], diff_lk1[l], diff_lq2[l], diff_lk2[l],
                            diff_subln[l], rel_bias, lambda_init)
        mix = jnp.concatenate([o_mla, o_gmlp, o_diff], axis=-1) @ w_o[l]
        x = layer_norm(DEEPNORM_ALPHA * x + mix, ln1_g[l], ln1_b[l])
        ffn = moe(x, router_w[l], router_bias[l], exp_w_gate[l], exp_w_up[l], exp_w_down[l],
                  shared_w_gate[l], shared_w_up[l], shared_w_down[l])
        x = layer_norm(DEEPNORM_ALPHA * x + ffn, ln2_g[l], ln2_b[l])
    return x
```

```python
import functools
import math

import numpy as np
import jax
import jax.numpy as jnp
from jax import lax
from jax.experimental import pallas as pl
from jax.experimental.pallas import tpu as pltpu

F32 = jnp.float32
BF16 = jnp.bfloat16

MLA_HEADS = 8
MLA_NOPE = 128
MLA_ROPE = 64
MLA_V = 128
ROPE_THETA = 10000.0
GMLP_GROUPS = 4
GMLP_CHUNK = 128
DIFF_HEADS = 4
DIFF_QK = 64
DIFF_V = 128
REL_BUCKETS = 32
REL_MAX_EXACT = 16
REL_MAX_DIST = 128
N_GROUPS = 8
TOPK_GROUPS = 4
TOP_K = 8
ROUTED_SCALE = 2.5

LANES = 128
VMEM_LIMIT_BYTES = 56 * 1024 * 1024

NEG_INF = float("-inf")


def _cparams(n_grid_dims):
    return pltpu.CompilerParams(
        dimension_semantics=("arbitrary",) * n_grid_dims,
        vmem_limit_bytes=VMEM_LIMIT_BYTES,
    )


def _const_spec(shape):
    zeros = (0,) * len(shape)
    return pl.BlockSpec(shape, lambda *_: zeros, pipeline_mode=pl.Buffered(1))


def _dot(a, b):
    return jnp.dot(a, b, preferred_element_type=F32)


def _dot_nt(a, b):
    return lax.dot_general(a, b, (((1,), (1,)), ((), ())), preferred_element_type=F32)


def _rms(x, g, eps=1e-6):
    ms = jnp.mean(jnp.square(x), axis=-1, keepdims=True)
    return x * lax.rsqrt(ms + eps) * g


def _layer_norm(x, g, b, eps=1e-5):
    mu = jnp.mean(x, axis=-1, keepdims=True)
    xc = x - mu
    var = jnp.mean(jnp.square(xc), axis=-1, keepdims=True)
    return xc * lax.rsqrt(var + eps) * g + b


def _swap_rope_halves(x):
    n = x.shape[-1]
    lane = lax.broadcasted_iota(jnp.int32, x.shape, x.ndim - 1) % LANES
    half = MLA_ROPE // 2
    return jnp.where(lane < half, pltpu.roll(x, n - half, x.ndim - 1), pltpu.roll(x, half, x.ndim - 1))


_C_Q = (0, 512)
_C_KV = (512, 768)
_C_KR = (768, 896)
_C_GU = (896, 1408)
_C_GV = (1408, 1920)
_C_DQ = (1920, 2432)
_C_DK = (2432, 2944)
_C_DV = (2944, 3456)
_WA_COLS = 3456


def _proj_in_kernel(x_ref, cos_ref, sin_ref, wa_ref, qg_ref, wuq_ref, kvg_ref, wukv_ref,
                    lng_ref, lnb_ref, ws_ref, bst_ref,
                    q_out, k_out, v_out, og_out, dq_out, dk_out, dv_out, *, tm):
    xb = x_ref[...].astype(BF16)

    def proj(cols):
        return _dot(xb, wa_ref[:, cols[0]:cols[1]])

    cos = cos_ref[...]
    sin = sin_ref[...]
    mla_scale = (MLA_NOPE + MLA_ROPE) ** -0.5
    nope_w = MLA_HEADS * MLA_NOPE

    q = _dot(_rms(proj(_C_Q), qg_ref[...]).astype(BF16), wuq_ref[...])
    q_rope = q[:, nope_w:]
    cos8 = jnp.concatenate([cos] * MLA_HEADS, axis=-1)
    sin8 = jnp.concatenate([sin] * MLA_HEADS, axis=-1)
    q_rope = q_rope * cos8 + _swap_rope_halves(q_rope) * sin8
    for h in range(MLA_HEADS):
        q_out[:, 2 * LANES * h:2 * LANES * h + LANES] = (q[:, LANES * h:LANES * (h + 1)] * mla_scale).astype(BF16)
        q_out[:, 2 * LANES * h + LANES:2 * LANES * (h + 1)] = (
            q_rope[:, LANES * h:LANES * (h + 1)] * mla_scale).astype(BF16)

    kv = _dot(_rms(proj(_C_KV), kvg_ref[...]).astype(BF16), wukv_ref[...])
    k_rope = proj(_C_KR)
    k_rope = (k_rope * cos + _swap_rope_halves(k_rope) * sin).astype(BF16)
    for h in range(MLA_HEADS):
        k_out[:, 2 * LANES * h:2 * LANES * h + LANES] = kv[:, LANES * h:LANES * (h + 1)].astype(BF16)
        k_out[:, 2 * LANES * h + LANES:2 * LANES * (h + 1)] = k_rope
    v_out[...] = kv[:, nope_w:].astype(BF16)

    u = jax.nn.gelu(proj(_C_GU))
    v = _layer_norm(jax.nn.gelu(proj(_C_GV)), lng_ref[...], lnb_ref[...])
    row = lax.broadcasted_iota(jnp.int32, (GMLP_CHUNK, GMLP_CHUNK), 0)
    col = lax.broadcasted_iota(jnp.int32, (GMLP_CHUNK, GMLP_CHUNK), 1)
    causal = col <= row
    for g in range(GMLP_GROUPS):
        ws = jnp.where(causal, ws_ref[g], 0.0).astype(BF16)
        bias = bst_ref[:, g:g + 1]
        for c in range(tm // GMLP_CHUNK):
            rows = slice(c * GMLP_CHUNK, (c + 1) * GMLP_CHUNK)
            cols = slice(g * LANES, (g + 1) * LANES)
            mixed = _dot(ws, v[rows, cols].astype(BF16)) + bias
            og_out[rows, cols] = (u[rows, cols] * mixed).astype(BF16)

    dq_out[...] = (proj(_C_DQ) * (DIFF_QK ** -0.5)).astype(BF16)
    dk_out[...] = proj(_C_DK).astype(BF16)
    dv_out[...] = proj(_C_DV).astype(BF16)


def _proj_in(x2, cos, sin, wa, qg, wuq, kvg, wukv, lng, lnb, ws, bst, *, seq, tm):
    T, D = x2.shape
    n_pos_blocks = seq // tm
    row = lambda i: (i, 0)
    pos = lambda i: (i % n_pos_blocks, 0)
    widths = (2 * LANES * MLA_HEADS, 2 * LANES * MLA_HEADS, LANES * MLA_HEADS, 512, 512, 512, 512)
    return pl.pallas_call(
        functools.partial(_proj_in_kernel, tm=tm),
        grid=(T // tm,),
        in_specs=[
            pl.BlockSpec((tm, D), row),
            pl.BlockSpec((tm, LANES), pos),
            pl.BlockSpec((tm, LANES), pos),
            _const_spec(wa.shape), _const_spec(qg.shape), _const_spec(wuq.shape),
            _const_spec(kvg.shape), _const_spec(wukv.shape),
            _const_spec(lng.shape), _const_spec(lnb.shape), _const_spec(ws.shape), _const_spec(bst.shape),
        ],
        out_specs=[pl.BlockSpec((tm, w), row) for w in widths],
        out_shape=[jax.ShapeDtypeStruct((T, w), BF16) for w in widths],
        compiler_params=_cparams(1),
        name="proj_in",
    )(x2, cos, sin, wa, qg, wuq, kvg, wukv, lng, lnb, ws, bst)


def _causal_pairs(n_blocks):
    qi, ki = [], []
    for q in range(n_blocks):
        for k in range(q + 1):
            qi.append(q)
            ki.append(k)
    return np.asarray(qi, np.int32), np.asarray(ki, np.int32)


def _online_softmax_step(s, v, m_ref, l_ref, acc_ref):
    m_prev = m_ref[...]
    m_new = jnp.maximum(m_prev, jnp.max(s, axis=1, keepdims=True))
    alpha = jnp.exp(m_prev - m_new)
    p = jnp.exp(s - m_new)
    l_ref[...] = alpha * l_ref[...] + jnp.sum(p, axis=1, keepdims=True)
    acc_ref[...] = alpha * acc_ref[...] + _dot(p.astype(BF16), v)
    m_ref[...] = m_new


def _mla_attn_kernel(qi_ref, ki_ref, q_ref, k_ref, v_ref, o_ref, m_scr, l_scr, acc_scr, *, tb):
    p = pl.program_id(1)
    qi = qi_ref[p]
    ki = ki_ref[p]

    @pl.when(ki == 0)
    def _init():
        m_scr[...] = jnp.full(m_scr.shape, NEG_INF, F32)
        l_scr[...] = jnp.zeros(l_scr.shape, F32)
        acc_scr[...] = jnp.zeros(acc_scr.shape, F32)

    def step(on_diagonal):
        if on_diagonal:
            row = lax.broadcasted_iota(jnp.int32, (tb, tb), 0)
            col = lax.broadcasted_iota(jnp.int32, (tb, tb), 1)
            keep = col <= row
        for h in range(MLA_HEADS):
            qk = slice(2 * LANES * h, 2 * LANES * (h + 1))
            s = _dot_nt(q_ref[:, qk], k_ref[:, qk])
            if on_diagonal:
                s = jnp.where(keep, s, NEG_INF)
            _online_softmax_step(s, v_ref[:, LANES * h:LANES * (h + 1)], m_scr.at[h], l_scr.at[h], acc_scr.at[h])

    @pl.when(ki != qi)
    def _below():
        step(False)

    @pl.when(ki == qi)
    def _diag():
        step(True)
        for h in range(MLA_HEADS):
            o_ref[:, LANES * h:LANES * (h + 1)] = (acc_scr[h] / l_scr[h]).astype(BF16)


def _mla_attn(q, k, v, *, batch, seq, tb):
    T = q.shape[0]
    nb = seq // tb
    qi_tab, ki_tab = _causal_pairs(nb)
    qmap = lambda b, p, qi, ki: (b * nb + qi[p], 0)
    kmap = lambda b, p, qi, ki: (b * nb + ki[p], 0)
    grid_spec = pltpu.PrefetchScalarGridSpec(
        num_scalar_prefetch=2,
        grid=(batch, len(qi_tab)),
        in_specs=[
            pl.BlockSpec((tb, q.shape[1]), qmap),
            pl.BlockSpec((tb, k.shape[1]), kmap),
            pl.BlockSpec((tb, v.shape[1]), kmap),
        ],
        out_specs=pl.BlockSpec((tb, v.shape[1]), qmap),
        scratch_shapes=[
            pltpu.VMEM((MLA_HEADS, tb, 1), F32),
            pltpu.VMEM((MLA_HEADS, tb, 1), F32),
            pltpu.VMEM((MLA_HEADS, tb, MLA_V), F32),
        ],
    )
    return pl.pallas_call(
        functools.partial(_mla_attn_kernel, tb=tb),
        grid_spec=grid_spec,
        out_shape=jax.ShapeDtypeStruct((T, v.shape[1]), BF16),
        compiler_params=_cparams(2),
        name="mla_attn",
    )(jnp.asarray(qi_tab), jnp.asarray(ki_tab), q, k, v)


def _t5_causal_bucket(n):
    is_small = n < REL_MAX_EXACT
    nf = jnp.maximum(n, 1).astype(F32)
    large = REL_MAX_EXACT + (jnp.log(nf / REL_MAX_EXACT) / math.log(REL_MAX_DIST / REL_MAX_EXACT)
                             * (REL_BUCKETS - REL_MAX_EXACT)).astype(jnp.int32)
    large = jnp.minimum(large, REL_BUCKETS - 1)
    return jnp.where(is_small, n, large)


def _rel_bias_kernel(tab_ref, o_ref, *, tb):
    slab = pl.program_id(0)
    row = lax.broadcasted_iota(jnp.int32, (tb, tb), 0)
    col = lax.broadcasted_iota(jnp.int32, (tb, tb), 1)
    bucket = _t5_causal_bucket(jnp.maximum(slab * tb + row - col, 0))
    for h in range(DIFF_HEADS):
        far = tab_ref[REL_BUCKETS - 1, h]
        acc = jnp.zeros((tb, tb), F32)
        for b in range(REL_BUCKETS - 1):
            acc = jnp.where(bucket == b, tab_ref[b, h] - far, acc)
        o_ref[h, 0] = acc


def _rel_bias_tiles(rel_bias, *, tb):
    return pl.pallas_call(
        functools.partial(_rel_bias_kernel, tb=tb),
        grid=(2,),
        in_specs=[pl.BlockSpec(memory_space=pltpu.SMEM)],
        out_specs=pl.BlockSpec((DIFF_HEADS, 1, tb, tb), lambda s: (0, s, 0, 0)),
        out_shape=jax.ShapeDtypeStruct((DIFF_HEADS, 2, tb, tb), F32),
        compiler_params=_cparams(1),
        name="rel_bias",
    )(rel_bias)


def _diff_attn_kernel(qi_ref, ki_ref, q_ref, k_ref, v_ref, bias_ref, lq1_ref, lk1_ref, lq2_ref, lk2_ref, g_ref,
                      o_ref, m_scr, l_scr, acc_scr, *, tb, lambda_init):
    p = pl.program_id(1)
    qi = qi_ref[p]
    ki = ki_ref[p]

    @pl.when(ki == 0)
    def _init():
        m_scr[...] = jnp.full(m_scr.shape, NEG_INF, F32)
        l_scr[...] = jnp.zeros(l_scr.shape, F32)
        acc_scr[...] = jnp.zeros(acc_scr.shape, F32)

    def step(slab):
        if slab == 0:
            row = lax.broadcasted_iota(jnp.int32, (tb, tb), 0)
            col = lax.broadcasted_iota(jnp.int32, (tb, tb), 1)
            keep = col <= row
        lane = lax.broadcasted_iota(jnp.int32, (tb, LANES), 1)
        for h in range(DIFF_HEADS):
            hs = slice(LANES * h, LANES * (h + 1))
            qh = q_ref[:, hs]
            kh = k_ref[:, hs]
            vh = v_ref[:, hs]
            for m in range(2):
                in_map = (lane < DIFF_QK) if m == 0 else (lane >= DIFF_QK)
                s = _dot_nt(jnp.where(in_map, qh, jnp.zeros_like(qh)), kh)
                if slab is not None:
                    s = s + bias_ref[h, slab]
                if slab == 0:
                    s = jnp.where(keep, s, NEG_INF)
                i = 2 * h + m
                _online_softmax_step(s, vh, m_scr.at[i], l_scr.at[i], acc_scr.at[i])

    @pl.when(ki < qi - 1)
    def _far():
        step(None)

    @pl.when(ki == qi - 1)
    def _near():
        step(1)

    @pl.when(ki == qi)
    def _diag():
        step(0)
        lam = (jnp.exp(jnp.sum(lq1_ref[...] * lk1_ref[...], axis=-1, keepdims=True))
               - jnp.exp(jnp.sum(lq2_ref[...] * lk2_ref[...], axis=-1, keepdims=True)) + lambda_init)
        for h in range(DIFF_HEADS):
            o = acc_scr[2 * h] / l_scr[2 * h] - lam * (acc_scr[2 * h + 1] / l_scr[2 * h + 1])
            o = _rms(o, g_ref[...]) * (1.0 - lambda_init)
            o_ref[:, LANES * h:LANES * (h + 1)] = o.astype(BF16)


def _diff_attn(q, k, v, bias_tiles, lq1, lk1, lq2, lk2, subln, *, batch, seq, tb, lambda_init):
    T, W = q.shape
    nb = seq // tb
    qi_tab, ki_tab = _causal_pairs(nb)
    qmap = lambda b, p, qi, ki: (b * nb + qi[p], 0)
    kmap = lambda b, p, qi, ki: (b * nb + ki[p], 0)
    small = lambda a: _const_spec(a.shape)
    grid_spec = pltpu.PrefetchScalarGridSpec(
        num_scalar_prefetch=2,
        grid=(batch, len(qi_tab)),
        in_specs=[
            pl.BlockSpec((tb, W), qmap),
            pl.BlockSpec((tb, W), kmap),
            pl.BlockSpec((tb, W), kmap),
            small(bias_tiles), small(lq1), small(lk1), small(lq2), small(lk2), small(subln),
        ],
        out_specs=pl.BlockSpec((tb, W), qmap),
        scratch_shapes=[
            pltpu.VMEM((2 * DIFF_HEADS, tb, 1), F32),
            pltpu.VMEM((2 * DIFF_HEADS, tb, 1), F32),
            pltpu.VMEM((2 * DIFF_HEADS, tb, DIFF_V), F32),
        ],
    )
    return pl.pallas_call(
        functools.partial(_diff_attn_kernel, tb=tb, lambda_init=lambda_init),
        grid_spec=grid_spec,
        out_shape=jax.ShapeDtypeStruct((T, W), BF16),
        compiler_params=_cparams(2),
        name="diff_attn",
    )(jnp.asarray(qi_tab), jnp.asarray(ki_tab), q, k, v, bias_tiles, lq1, lk1, lq2, lk2, subln)


ROW_CHUNKS = 16


def _split_bf16(x):
    hi = x.astype(BF16)
    lo = (x - hi.astype(F32)).astype(BF16)
    return hi, lo


def _store_row_slabs(dst_ref, val, tm):
    for c in range(ROW_CHUNKS):
        dst_ref[pl.ds(c, tm, stride=ROW_CHUNKS), :] = val[:, LANES * c:LANES * (c + 1)]


def _load_row_slabs(src_ref, tm):
    return jnp.concatenate([src_ref[pl.ds(c, tm, stride=ROW_CHUNKS), :] for c in range(ROW_CHUNKS)], axis=-1)


def _proj_out_kernel(om_ref, og_ref, od_ref, x_ref, wo_ref, g_ref, b_ref, rwh_ref, rwl_ref,
                     x1_out, x1s_out, logit_out, *, tm, alpha):
    n_m = om_ref.shape[1]
    n_g = og_ref.shape[1]
    mix = (_dot(om_ref[...], wo_ref[0:n_m, :])
           + _dot(og_ref[...], wo_ref[n_m:n_m + n_g, :])
           + _dot(od_ref[...], wo_ref[n_m + n_g:, :]))
    x1 = _layer_norm(alpha * x_ref[...] + mix, g_ref[...], b_ref[...])
    x1_out[...] = x1
    _store_row_slabs(x1s_out, x1, tm)
    xh, xl = _split_bf16(x1)
    logit_out[...] = _dot_nt(rwh_ref[...], xh) + _dot_nt(rwh_ref[...], xl) + _dot_nt(rwl_ref[...], xh)


def _proj_out(om, og, od, x2, wo, g, b, rwh, rwl, *, tm, alpha):
    T, D = x2.shape
    E = rwh.shape[0]
    row = lambda i: (i, 0)
    return pl.pallas_call(
        functools.partial(_proj_out_kernel, tm=tm, alpha=alpha),
        grid=(T // tm,),
        in_specs=[
            pl.BlockSpec((tm, om.shape[1]), row), pl.BlockSpec((tm, og.shape[1]), row),
            pl.BlockSpec((tm, od.shape[1]), row), pl.BlockSpec((tm, D), row),
            _const_spec(wo.shape), _const_spec(g.shape), _const_spec(b.shape),
            _const_spec(rwh.shape), _const_spec(rwl.shape),
        ],
        out_specs=[
            pl.BlockSpec((tm, D), row),
            pl.BlockSpec((tm * ROW_CHUNKS, LANES), row),
            pl.BlockSpec((E, tm), lambda i: (0, i)),
        ],
        out_shape=[
            jax.ShapeDtypeStruct((T, D), F32),
            jax.ShapeDtypeStruct((T * ROW_CHUNKS, LANES), F32),
            jax.ShapeDtypeStruct((E, T), F32),
        ],
        compiler_params=_cparams(1),
        name="proj_out",
    )(om, og, od, x2, wo, g, b, rwh, rwl)


def _first_argmax(v, axis, size):
    mx = jnp.max(v, axis=axis, keepdims=True)
    idx = lax.broadcasted_iota(jnp.int32, v.shape, axis)
    first = jnp.min(jnp.where(v == mx, idx, size), axis=axis, keepdims=True)
    return mx, first


def _route_kernel(logit_ref, bias_ref, gate_out, sel_out):
    E, tn = logit_ref.shape
    per_group = E // N_GROUPS
    scores = jax.nn.sigmoid(logit_ref[...])
    biased = scores + bias_ref[...]
    grouped = biased.reshape(N_GROUPS, per_group, tn)
    member = lax.broadcasted_iota(jnp.int32, grouped.shape, 1)
    top1, first = _first_argmax(grouped, 1, per_group)
    top2 = jnp.max(jnp.where(member == first, NEG_INF, grouped), axis=1, keepdims=True)
    group_score = (top1 + top2).reshape(N_GROUPS, tn)

    group_id = lax.broadcasted_iota(jnp.int32, group_score.shape, 0)
    group_sel = jnp.zeros(group_score.shape, F32)
    remaining = group_score
    for _ in range(TOPK_GROUPS):
        _, first = _first_argmax(remaining, 0, N_GROUPS)
        hit = group_id == first
        group_sel = jnp.where(hit, 1.0, group_sel)
        remaining = jnp.where(hit, NEG_INF, remaining)

    masked = jnp.where(group_sel.reshape(N_GROUPS, 1, tn) > 0.5, grouped, NEG_INF).reshape(E, tn)
    expert_id = lax.broadcasted_iota(jnp.int32, masked.shape, 0)
    sel = jnp.zeros(masked.shape, F32)
    for _ in range(TOP_K):
        _, first = _first_argmax(masked, 0, E)
        hit = expert_id == first
        sel = jnp.where(hit, 1.0, sel)
        masked = jnp.where(hit, NEG_INF, masked)

    w = scores * sel
    gate_out[...] = w / jnp.sum(w, axis=0, keepdims=True) * ROUTED_SCALE
    sel_out[...] = sel


def _route(logits_t, bias_col, *, tn):
    E, T = logits_t.shape
    col = lambda i: (0, i)
    return pl.pallas_call(
        _route_kernel,
        grid=(T // tn,),
        in_specs=[pl.BlockSpec((E, tn), col), _const_spec(bias_col.shape)],
        out_specs=[pl.BlockSpec((E, tn), col), pl.BlockSpec((E, tn), col)],
        out_shape=[jax.ShapeDtypeStruct((E, T), F32), jax.ShapeDtypeStruct((E, T), F32)],
        compiler_params=_cparams(1),
        name="route",
    )(logits_t, bias_col)


def _dispatch_plan(gates_t, sel_t, *, tm, n_tiles):
    E, T = sel_t.shape
    sel = sel_t > 0.5
    seli = sel.astype(jnp.int32)
    counts = jnp.sum(seli, axis=1)
    padded = (counts + tm - 1) // tm * tm
    group_end = jnp.cumsum(padded)
    group_start = group_end - padded
    count_start = jnp.cumsum(counts) - counts
    flat_idx = jnp.nonzero(sel.reshape(-1), size=T * TOP_K, fill_value=0)[0].astype(jnp.int32)
    slot_t = jnp.cumsum(seli, axis=0) - seli

    tile_start = jnp.arange(n_tiles, dtype=jnp.int32) * tm
    tile_expert = jnp.searchsorted(group_end, tile_start, side="right").astype(jnp.int32)
    tile_valid = (tile_expert < E).astype(jnp.int32)
    last_expert = jnp.max(jnp.where(counts > 0, jnp.arange(E, dtype=jnp.int32), 0))
    tile_expert = jnp.where(tile_valid > 0, tile_expert, last_expert)

    r = jnp.arange(n_tiles * tm, dtype=jnp.int32)
    e_r = tile_expert[r // tm]
    k_r = r - group_start[e_r]
    valid = jnp.logical_and(tile_valid[r // tm] > 0, k_r < counts[e_r])
    pair = flat_idx[jnp.clip(count_start[e_r] + k_r, 0, T * TOP_K - 1)]
    t_r = jnp.where(valid, pair % T, 0)
    pair = e_r * T + t_r
    gate_r = jnp.where(valid, gates_t.reshape(-1)[pair], 0.0)
    slot_r = slot_t.reshape(-1)[pair]
    dest_r = jnp.where(valid, slot_r * T + t_r, TOP_K * T + r % tm)
    src_off = (t_r * ROW_CHUNKS).reshape(n_tiles, 1, tm)
    dst_off = (dest_r * ROW_CHUNKS).reshape(n_tiles, 1, tm)
    return src_off, dst_off, gate_r.reshape(-1, 1), tile_expert, tile_valid


def _experts_kernel(te_ref, tv_ref, src_ref, dst_ref, gate_ref, xs_hbm, wg_ref, wu_ref, wd_ref, out_hbm,
                    xbuf, ybuf, wgb, wub, wdb, sems, *, tm):
    i = pl.program_id(0)
    slab = tm * ROW_CHUNKS

    @pl.when(i == 0)
    def _zero_spare_slabs():
        ybuf[...] = jnp.zeros(ybuf.shape, F32)
        spare = pltpu.make_async_copy(ybuf, out_hbm.at[pl.ds(out_hbm.shape[0] - slab, slab)], sems.at[1])
        spare.start()
        spare.wait()

    @pl.when(tv_ref[i] > 0)
    def _tile():
        def gather_row(r, carry):
            pltpu.make_async_copy(
                xs_hbm.at[pl.ds(pl.multiple_of(src_ref[0, 0, r], ROW_CHUNKS), ROW_CHUNKS)],
                xbuf.at[pl.ds(pl.multiple_of(r * ROW_CHUNKS, ROW_CHUNKS), ROW_CHUNKS)],
                sems.at[0]).start()
            return carry
        lax.fori_loop(0, tm, gather_row, 0)

        @pl.when(jnp.logical_or(i == 0, te_ref[i] != te_ref[jnp.maximum(i - 1, 0)]))
        def _cast_weights():
            wgb[...] = wg_ref[0, 0].astype(BF16)
            wub[...] = wu_ref[0, 0].astype(BF16)
            wdb[...] = wd_ref[0, 0].astype(BF16)

        pltpu.make_async_copy(xs_hbm.at[pl.ds(0, slab)], xbuf, sems.at[0]).wait()
        xb = _load_row_slabs(xbuf, tm).astype(BF16)
        g = _dot(xb, wgb[...])
        u = _dot(xb, wub[...])
        h = (jax.nn.silu(g) * u * gate_ref[...]).astype(BF16)
        y = _dot(h, wdb[...])
        _store_row_slabs(ybuf, y, tm)

        def scatter_row(r, carry):
            pltpu.make_async_copy(
                ybuf.at[pl.ds(pl.multiple_of(r * ROW_CHUNKS, ROW_CHUNKS), ROW_CHUNKS)],
                out_hbm.at[pl.ds(pl.multiple_of(dst_ref[0, 0, r], ROW_CHUNKS), ROW_CHUNKS)],
                sems.at[1]).start()
            return carry
        lax.fori_loop(0, tm, scatter_row, 0)
        pltpu.make_async_copy(ybuf, out_hbm.at[pl.ds(0, slab)], sems.at[1]).wait()


def _experts(tile_expert, tile_valid, src_off, dst_off, gate_r, x1s, wg, wu, wd, layer, *, tm, n_tokens):
    n_tiles = tile_expert.shape[0]
    D, F = wg.shape[2], wg.shape[3]
    smem_rows = pl.BlockSpec((1, 1, tm), lambda i, te, tv: (i, 0, 0), memory_space=pltpu.SMEM)
    wmap = lambda i, te, tv: (layer, te[i], 0, 0)
    grid_spec = pltpu.PrefetchScalarGridSpec(
        num_scalar_prefetch=2,
        grid=(n_tiles,),
        in_specs=[
            smem_rows, smem_rows,
            pl.BlockSpec((tm, 1), lambda i, te, tv: (i, 0)),
            pl.BlockSpec(memory_space=pl.ANY),
            pl.BlockSpec((1, 1, D, F), wmap),
            pl.BlockSpec((1, 1, D, F), wmap),
            pl.BlockSpec((1, 1, F, D), wmap),
        ],
        out_specs=pl.BlockSpec(memory_space=pl.ANY),
        scratch_shapes=[
            pltpu.VMEM((tm * ROW_CHUNKS, LANES), F32),
            pltpu.VMEM((tm * ROW_CHUNKS, LANES), F32),
            pltpu.VMEM((D, F), BF16),
            pltpu.VMEM((D, F), BF16),
            pltpu.VMEM((F, D), BF16),
            pltpu.SemaphoreType.DMA((2,)),
        ],
    )
    out_rows = (TOP_K * n_tokens + tm) * ROW_CHUNKS
    return pl.pallas_call(
        functools.partial(_experts_kernel, tm=tm),
        grid_spec=grid_spec,
        out_shape=jax.ShapeDtypeStruct((out_rows, LANES), F32),
        compiler_params=pltpu.CompilerParams(
            dimension_semantics=("arbitrary",), vmem_limit_bytes=VMEM_LIMIT_BYTES, has_side_effects=True),
        name="experts",
    )(tile_expert, tile_valid, src_off, dst_off, gate_r, x1s, wg, wu, wd)


def _ffn_out_kernel(x_ref, *rest, tm, alpha):
    slot_refs = rest[:TOP_K]
    sg_ref, su_ref, sd_ref, g_ref, b_ref, o_ref = rest[TOP_K:]
    x1 = x_ref[...]
    xb = x1.astype(BF16)
    hs = (jax.nn.silu(_dot(xb, sg_ref[...])) * _dot(xb, su_ref[...])).astype(BF16)
    ffn = _dot(hs, sd_ref[...])
    routed = _load_row_slabs(slot_refs[0], tm)
    for k in range(1, TOP_K):
        routed = routed + _load_row_slabs(slot_refs[k], tm)
    o_ref[...] = _layer_norm(alpha * x1 + (routed + ffn), g_ref[...], b_ref[...])


def _ffn_out(x1, routed_slabs, sg, su, sd, g, b, *, tm, alpha):
    T, D = x1.shape
    n_row_blocks = T // tm
    row = lambda i: (i, 0)
    slot_specs = [pl.BlockSpec((tm * ROW_CHUNKS, LANES), functools.partial(lambda i, k: (k * n_row_blocks + i, 0), k=k))
                  for k in range(TOP_K)]
    return pl.pallas_call(
        functools.partial(_ffn_out_kernel, tm=tm, alpha=alpha),
        grid=(n_row_blocks,),
        in_specs=[pl.BlockSpec((tm, D), row)] + slot_specs + [
            _const_spec(sg.shape), _const_spec(su.shape), _const_spec(sd.shape),
            _const_spec(g.shape), _const_spec(b.shape)],
        out_specs=pl.BlockSpec((tm, D), row),
        out_shape=jax.ShapeDtypeStruct((T, D), F32),
        compiler_params=_cparams(1),
        name="ffn_out",
    )(x1, *([routed_slabs] * TOP_K), sg, su, sd, g, b)


def _tile_sizes(batch, seq):
    T = batch * seq
    return dict(
        tm=min(256, seq),
        tb=min(512, seq),
        tn=min(512, T),
        te=min(256, T),
    )


def _rope_tables(seq):
    half = MLA_ROPE // 2
    inv = ROPE_THETA ** (-jnp.arange(half, dtype=F32) / half)
    ang = jnp.arange(seq, dtype=jnp.int32).astype(F32)[:, None] * inv[None, :]
    cos, sin = jnp.cos(ang), jnp.sin(ang)
    pad = jnp.zeros((seq, LANES - MLA_ROPE), F32)
    return jnp.concatenate([cos, cos, pad], axis=1), jnp.concatenate([-sin, sin, pad], axis=1)


def _pack_in_weight(w):
    D = w.shape[0]
    kr_end = 512 + 256 + MLA_ROPE
    return jnp.concatenate([w[:, :kr_end], jnp.zeros((D, LANES - MLA_ROPE), w.dtype), w[:, kr_end:]], axis=1).astype(BF16)


def _pack_uq(w):
    r = w.shape[0]
    w3 = w.reshape(r, MLA_HEADS, MLA_NOPE + MLA_ROPE)
    nope = w3[:, :, :MLA_NOPE].reshape(r, MLA_HEADS * MLA_NOPE)
    rope = jnp.pad(w3[:, :, MLA_NOPE:], ((0, 0), (0, 0), (0, LANES - MLA_ROPE))).reshape(r, MLA_HEADS * LANES)
    return jnp.concatenate([nope, rope], axis=1).astype(BF16)


def _pack_ukv(w):
    r = w.shape[0]
    w3 = w.reshape(r, MLA_HEADS, MLA_NOPE + MLA_V)
    return jnp.concatenate([w3[:, :, :MLA_NOPE].reshape(r, -1), w3[:, :, MLA_NOPE:].reshape(r, -1)], axis=1).astype(BF16)


def kernel(x, rel_bias, w_in, mla_q_norm, mla_w_uq, mla_kv_norm, mla_w_ukv, gmlp_ln_g, gmlp_ln_b, gmlp_w_s, gmlp_b_s, diff_lq1, diff_lk1, diff_lq2, diff_lk2, diff_subln, w_o, ln1_g, ln1_b, router_w, router_bias, exp_w_gate, exp_w_up, exp_w_down, shared_w_gate, shared_w_up, shared_w_down, ln2_g, ln2_b):
    B, S, D = x.shape
    T = B * S
    depth = w_in.shape[0]
    E = router_w.shape[-1]
    ts = _tile_sizes(B, S)
    alpha = (2 * depth) ** 0.25
    te = ts["te"]
    n_tiles = (T * TOP_K) // te + E

    cos, sin = _rope_tables(S)
    bias_tiles = _rel_bias_tiles(rel_bias, tb=ts["tb"])
    row2 = lambda a: a.reshape(1, -1)

    x2 = x.reshape(T, D)
    for l in range(depth):
        lambda_init = 0.8 - 0.6 * math.exp(-0.3 * l)
        q, k, v, og, dq, dk, dv = _proj_in(
            x2, cos, sin, _pack_in_weight(w_in[l]), row2(mla_q_norm[l]), _pack_uq(mla_w_uq[l]),
            row2(mla_kv_norm[l]), _pack_ukv(mla_w_ukv[l]), row2(gmlp_ln_g[l]), row2(gmlp_ln_b[l]),
            gmlp_w_s[l], gmlp_b_s[l].T, seq=S, tm=ts["tm"])
        om = _mla_attn(q, k, v, batch=B, seq=S, tb=ts["tb"])
        od = _diff_attn(dq, dk, dv, bias_tiles, row2(diff_lq1[l]), row2(diff_lk1[l]), row2(diff_lq2[l]),
                        row2(diff_lk2[l]), row2(diff_subln[l]), batch=B, seq=S, tb=ts["tb"], lambda_init=lambda_init)
        rwh, rwl = _split_bf16(router_w[l].T)
        x1, x1s, logits_t = _proj_out(om, og, od, x2, w_o[l].astype(BF16), row2(ln1_g[l]), row2(ln1_b[l]),
                                      rwh, rwl, tm=ts["tm"], alpha=alpha)
        gates_t, sel_t = _route(logits_t, router_bias[l].reshape(E, 1), tn=ts["tn"])
        src_off, dst_off, gate_r, tile_expert, tile_valid = _dispatch_plan(gates_t, sel_t, tm=te, n_tiles=n_tiles)
        routed = _experts(tile_expert, tile_valid, src_off, dst_off, gate_r, x1s,
                          exp_w_gate, exp_w_up, exp_w_down, l, tm=te, n_tokens=T)
        x2 = _ffn_out(x1, routed, shared_w_gate[l].astype(BF16), shared_w_up[l].astype(BF16),
                      shared_w_down[l].astype(BF16), row2(ln2_g[l]), row2(ln2_b[l]), tm=ts["tm"], alpha=alpha)
    return x2.reshape(B, S, D)
```

```python
import functools
import math

import numpy as np
import jax
import jax.numpy as jnp
from jax import lax
from jax.experimental import pallas as pl
from jax.experimental.pallas import tpu as pltpu

F32 = jnp.float32
BF16 = jnp.bfloat16

MLA_HEADS = 8
MLA_NOPE = 128
MLA_ROPE = 64
MLA_V = 128
ROPE_THETA = 10000.0
GMLP_GROUPS = 4
GMLP_CHUNK = 128
DIFF_HEADS = 4
DIFF_QK = 64
DIFF_V = 128
REL_BUCKETS = 32
REL_MAX_EXACT = 16
REL_MAX_DIST = 128
N_GROUPS = 8
TOPK_GROUPS = 4
TOP_K = 8
ROUTED_SCALE = 2.5

LANES = 128
VMEM_LIMIT_BYTES = 56 * 1024 * 1024

NEG_INF = float("-inf")


def _cparams(n_grid_dims):
    return pltpu.CompilerParams(
        dimension_semantics=("arbitrary",) * n_grid_dims,
        vmem_limit_bytes=VMEM_LIMIT_BYTES,
    )


def _const_spec(shape):
    zeros = (0,) * len(shape)
    return pl.BlockSpec(shape, lambda *_: zeros, pipeline_mode=pl.Buffered(1))


def _dot(a, b):
    return jnp.dot(a, b, preferred_element_type=F32)


def _dot_nt(a, b):
    return lax.dot_general(a, b, (((1,), (1,)), ((), ())), preferred_element_type=F32)


def _rms(x, g, eps=1e-6):
    ms = jnp.mean(jnp.square(x), axis=-1, keepdims=True)
    return x * lax.rsqrt(ms + eps) * g


def _layer_norm(x, g, b, eps=1e-5):
    mu = jnp.mean(x, axis=-1, keepdims=True)
    xc = x - mu
    var = jnp.mean(jnp.square(xc), axis=-1, keepdims=True)
    return xc * lax.rsqrt(var + eps) * g + b


def _swap_rope_halves(x):
    n = x.shape[-1]
    lane = lax.broadcasted_iota(jnp.int32, x.shape, x.ndim - 1) % LANES
    half = MLA_ROPE // 2
    return jnp.where(lane < half, pltpu.roll(x, n - half, x.ndim - 1), pltpu.roll(x, half, x.ndim - 1))


_C_Q = (0, 512)
_C_KV = (512, 768)
_C_KR = (768, 896)
_C_GU = (896, 1408)
_C_GV = (1408, 1920)
_C_DQ = (1920, 2432)
_C_DK = (2432, 2944)
_C_DV = (2944, 3456)
_WA_COLS = 3456


def _proj_in_kernel(x_ref, cos_ref, sin_ref, wa_ref, qg_ref, wuq_ref, kvg_ref, wukv_ref,
                    lng_ref, lnb_ref, ws_ref, bst_ref,
                    q_out, k_out, v_out, og_out, dq_out, dk_out, dv_out, *, tm):
    xb = x_ref[...].astype(BF16)

    def proj(cols):
        return _dot(xb, wa_ref[:, cols[0]:cols[1]])

    cos = cos_ref[...]
    sin = sin_ref[...]
    mla_scale = (MLA_NOPE + MLA_ROPE) ** -0.5
    nope_w = MLA_HEADS * MLA_NOPE

    q = _dot(_rms(proj(_C_Q), qg_ref[...]).astype(BF16), wuq_ref[...])
    q_rope = q[:, nope_w:]
    cos8 = jnp.concatenate([cos] * MLA_HEADS, axis=-1)
    sin8 = jnp.concatenate([sin] * MLA_HEADS, axis=-1)
    q_rope = q_rope * cos8 + _swap_rope_halves(q_rope) * sin8
    for h in range(MLA_HEADS):
        q_out[:, 2 * LANES * h:2 * LANES * h + LANES] = (q[:, LANES * h:LANES * (h + 1)] * mla_scale).astype(BF16)
        q_out[:, 2 * LANES * h + LANES:2 * LANES * (h + 1)] = (
            q_rope[:, LANES * h:LANES * (h + 1)] * mla_scale).astype(BF16)

    kv = _dot(_rms(proj(_C_KV), kvg_ref[...]).astype(BF16), wukv_ref[...])
    k_rope = proj(_C_KR)
    k_rope = (k_rope * cos + _swap_rope_halves(k_rope) * sin).astype(BF16)
    for h in range(MLA_HEADS):
        k_out[:, 2 * LANES * h:2 * LANES * h + LANES] = kv[:, LANES * h:LANES * (h + 1)].astype(BF16)
        k_out[:, 2 * LANES * h + LANES:2 * LANES * (h + 1)] = k_rope
    v_out[...] = kv[:, nope_w:].astype(BF16)

    u = jax.nn.gelu(proj(_C_GU))
    v = _layer_norm(jax.nn.gelu(proj(_C_GV)), lng_ref[...], lnb_ref[...])
    row = lax.broadcasted_iota(jnp.int32, (GMLP_CHUNK, GMLP_CHUNK), 0)
    col = lax.broadcasted_iota(jnp.int32, (GMLP_CHUNK, GMLP_CHUNK), 1)
    causal = col <= row
    for g in range(GMLP_GROUPS):
        ws = jnp.where(causal, ws_ref[g], 0.0).astype(BF16)
        bias = bst_ref[:, g:g + 1]
        for c in range(tm // GMLP_CHUNK):
            rows = slice(c * GMLP_CHUNK, (c + 1) * GMLP_CHUNK)
            cols = slice(g * LANES, (g + 1) * LANES)
            mixed = _dot(ws, v[rows, cols].astype(BF16)) + bias
            og_out[rows, cols] = (u[rows, cols] * mixed).astype(BF16)

    dq_out[...] = (proj(_C_DQ) * (DIFF_QK ** -0.5)).astype(BF16)
    dk_out[...] = proj(_C_DK).astype(BF16)
    dv_out[...] = proj(_C_DV).astype(BF16)


def _proj_in(x2, cos, sin, wa, qg, wuq, kvg, wukv, lng, lnb, ws, bst, *, seq, tm):
    T, D = x2.shape
    n_pos_blocks = seq // tm
    row = lambda i: (i, 0)
    pos = lambda i: (i % n_pos_blocks, 0)
    widths = (2 * LANES * MLA_HEADS, 2 * LANES * MLA_HEADS, LANES * MLA_HEADS, 512, 512, 512, 512)
    return pl.pallas_call(
        functools.partial(_proj_in_kernel, tm=tm),
        grid=(T // tm,),
        in_specs=[
            pl.BlockSpec((tm, D), row),
            pl.BlockSpec((tm, LANES), pos),
            pl.BlockSpec((tm, LANES), pos),
            _const_spec(wa.shape), _const_spec(qg.shape), _const_spec(wuq.shape),
            _const_spec(kvg.shape), _const_spec(wukv.shape),
            _const_spec(lng.shape), _const_spec(lnb.shape), _const_spec(ws.shape), _const_spec(bst.shape),
        ],
        out_specs=[pl.BlockSpec((tm, w), row) for w in widths],
        out_shape=[jax.ShapeDtypeStruct((T, w), BF16) for w in widths],
        compiler_params=_cparams(1),
        name="proj_in",
    )(x2, cos, sin, wa, qg, wuq, kvg, wukv, lng, lnb, ws, bst)


def _causal_pairs(n_blocks):
    qi, ki = [], []
    for q in range(n_blocks):
        for k in range(q + 1):
            qi.append(q)
            ki.append(k)
    return np.asarray(qi, np.int32), np.asarray(ki, np.int32)


def _online_softmax_step(s, v, m_ref, l_ref, acc_ref):
    m_prev = m_ref[...]
    m_new = jnp.maximum(m_prev, jnp.max(s, axis=1, keepdims=True))
    alpha = jnp.exp(m_prev - m_new)
    p = jnp.exp(s - m_new)
    l_ref[...] = alpha * l_ref[...] + jnp.sum(p, axis=1, keepdims=True)
    acc_ref[...] = alpha * acc_ref[...] + _dot(p.astype(BF16), v)
    m_ref[...] = m_new


def _mla_attn_kernel(qi_ref, ki_ref, q_ref, k_ref, v_ref, o_ref, m_scr, l_scr, acc_scr, *, tb):
    p = pl.program_id(1)
    qi = qi_ref[p]
    ki = ki_ref[p]

    @pl.when(ki == 0)
    def _init():
        m_scr[...] = jnp.full(m_scr.shape, NEG_INF, F32)
        l_scr[...] = jnp.zeros(l_scr.shape, F32)
        acc_scr[...] = jnp.zeros(acc_scr.shape, F32)

    def step(on_diagonal):
        if on_diagonal:
            row = lax.broadcasted_iota(jnp.int32, (tb, tb), 0)
            col = lax.broadcasted_iota(jnp.int32, (tb, tb), 1)
            keep = col <= row
        for h in range(MLA_HEADS):
            qk = slice(2 * LANES * h, 2 * LANES * (h + 1))
            s = _dot_nt(q_ref[:, qk], k_ref[:, qk])
            if on_diagonal:
                s = jnp.where(keep, s, NEG_INF)
            _online_softmax_step(s, v_ref[:, LANES * h:LANES * (h + 1)], m_scr.at[h], l_scr.at[h], acc_scr.at[h])

    @pl.when(ki != qi)
    def _below():
        step(False)

    @pl.when(ki == qi)
    def _diag():
        step(True)
        for h in range(MLA_HEADS):
            o_ref[:, LANES * h:LANES * (h + 1)] = (acc_scr[h] / l_scr[h]).astype(BF16)


def _mla_attn(q, k, v, *, batch, seq, tb):
    T = q.shape[0]
    nb = seq // tb
    qi_tab, ki_tab = _causal_pairs(nb)
    qmap = lambda b, p, qi, ki: (b * nb + qi[p], 0)
    kmap = lambda b, p, qi, ki: (b * nb + ki[p], 0)
    grid_spec = pltpu.PrefetchScalarGridSpec(
        num_scalar_prefetch=2,
        grid=(batch, len(qi_tab)),
        in_specs=[
            pl.BlockSpec((tb, q.shape[1]), qmap),
            pl.BlockSpec((tb, k.shape[1]), kmap),
            pl.BlockSpec((tb, v.shape[1]), kmap),
        ],
        out_specs=pl.BlockSpec((tb, v.shape[1]), qmap),
        scratch_shapes=[
            pltpu.VMEM((MLA_HEADS, tb, 1), F32),
            pltpu.VMEM((MLA_HEADS, tb, 1), F32),
            pltpu.VMEM((MLA_HEADS, tb, MLA_V), F32),
        ],
    )
    return pl.pallas_call(
        functools.partial(_mla_attn_kernel, tb=tb),
        grid_spec=grid_spec,
        out_shape=jax.ShapeDtypeStruct((T, v.shape[1]), BF16),
        compiler_params=_cparams(2),
        name="mla_attn",
    )(jnp.asarray(qi_tab), jnp.asarray(ki_tab), q, k, v)


def _t5_causal_bucket(n):
    is_small = n < REL_MAX_EXACT
    nf = jnp.maximum(n, 1).astype(F32)
    large = REL_MAX_EXACT + (jnp.log(nf / REL_MAX_EXACT) / math.log(REL_MAX_DIST / REL_MAX_EXACT)
                             * (REL_BUCKETS - REL_MAX_EXACT)).astype(jnp.int32)
    large = jnp.minimum(large, REL_BUCKETS - 1)
    return jnp.where(is_small, n, large)


def _rel_bias_kernel(tab_ref, o_ref, *, tb):
    slab = pl.program_id(0)
    row = lax.broadcasted_iota(jnp.int32, (tb, tb), 0)
    col = lax.broadcasted_iota(jnp.int32, (tb, tb), 1)
    bucket = _t5_causal_bucket(jnp.maximum(slab * tb + row - col, 0))
    for h in range(DIFF_HEADS):
        far = tab_ref[REL_BUCKETS - 1, h]
        acc = jnp.zeros((tb, tb), F32)
        for b in range(REL_BUCKETS - 1):
            acc = jnp.where(bucket == b, tab_ref[b, h] - far, acc)
        o_ref[h, 0] = acc


def _rel_bias_tiles(rel_bias, *, tb):
    return pl.pallas_call(
        functools.partial(_rel_bias_kernel, tb=tb),
        grid=(2,),
        in_specs=[pl.BlockSpec(memory_space=pltpu.SMEM)],
        out_specs=pl.BlockSpec((DIFF_HEADS, 1, tb, tb), lambda s: (0, s, 0, 0)),
        out_shape=jax.ShapeDtypeStruct((DIFF_HEADS, 2, tb, tb), F32),
        compiler_params=_cparams(1),
        name="rel_bias",
    )(rel_bias)


def _diff_attn_kernel(qi_ref, ki_ref, q_ref, k_ref, v_ref, bias_ref, lq1_ref, lk1_ref, lq2_ref, lk2_ref, g_ref,
                      o_ref, m_scr, l_scr, acc_scr, *, tb, lambda_init):
    p = pl.program_id(1)
    qi = qi_ref[p]
    ki = ki_ref[p]

    @pl.when(ki == 0)
    def _init():
        m_scr[...] = jnp.full(m_scr.shape, NEG_INF, F32)
        l_scr[...] = jnp.zeros(l_scr.shape, F32)
        acc_scr[...] = jnp.zeros(acc_scr.shape, F32)

    def step(slab):
        if slab == 0:
            row = lax.broadcasted_iota(jnp.int32, (tb, tb), 0)
            col = lax.broadcasted_iota(jnp.int32, (tb, tb), 1)
            keep = col <= row
        lane = lax.broadcasted_iota(jnp.int32, (tb, LANES), 1)
        for h in range(DIFF_HEADS):
            hs = slice(LANES * h, LANES * (h + 1))
            qh = q_ref[:, hs]
            kh = k_ref[:, hs]
            vh = v_ref[:, hs]
            for m in range(2):
                in_map = (lane < DIFF_QK) if m == 0 else (lane >= DIFF_QK)
                s = _dot_nt(jnp.where(in_map, qh, jnp.zeros_like(qh)), kh)
                if slab is not None:
                    s = s + bias_ref[h, slab]
                if slab == 0:
                    s = jnp.where(keep, s, NEG_INF)
                i = 2 * h + m
                _online_softmax_step(s, vh, m_scr.at[i], l_scr.at[i], acc_scr.at[i])

    @pl.when(ki < qi - 1)
    def _far():
        step(None)

    @pl.when(ki == qi - 1)
    def _near():
        step(1)

    @pl.when(ki == qi)
    def _diag():
        step(0)
        lam = (jnp.exp(jnp.sum(lq1_ref[...] * lk1_ref[...], axis=-1, keepdims=True))
               - jnp.exp(jnp.sum(lq2_ref[...] * lk2_ref[...], axis=-1, keepdims=True)) + lambda_init)
        for h in range(DIFF_HEADS):
            o = acc_scr[2 * h] / l_scr[2 * h] - lam * (acc_scr[2 * h + 1] / l_scr[2 * h + 1])
            o = _rms(o, g_ref[...]) * (1.0 - lambda_init)
            o_ref[:, LANES * h:LANES * (h + 1)] = o.astype(BF16)


def _diff_attn(q, k, v, bias_tiles, lq1, lk1, lq2, lk2, subln, *, batch, seq, tb, lambda_init):
    T, W = q.shape
    nb = seq // tb
    qi_tab, ki_tab = _causal_pairs(nb)
    qmap = lambda b, p, qi, ki: (b * nb + qi[p], 0)
    kmap = lambda b, p, qi, ki: (b * nb + ki[p], 0)
    small = lambda a: _const_spec(a.shape)
    grid_spec = pltpu.PrefetchScalarGridSpec(
        num_scalar_prefetch=2,
        grid=(batch, len(qi_tab)),
        in_specs=[
            pl.BlockSpec((tb, W), qmap),
            pl.BlockSpec((tb, W), kmap),
            pl.BlockSpec((tb, W), kmap),
            small(bias_tiles), small(lq1), small(lk1), small(lq2), small(lk2), small(subln),
        ],
        out_specs=pl.BlockSpec((tb, W), qmap),
        scratch_shapes=[
            pltpu.VMEM((2 * DIFF_HEADS, tb, 1), F32),
            pltpu.VMEM((2 * DIFF_HEADS, tb, 1), F32),
            pltpu.VMEM((2 * DIFF_HEADS, tb, DIFF_V), F32),
        ],
    )
    return pl.pallas_call(
        functools.partial(_diff_attn_kernel, tb=tb, lambda_init=lambda_init),
        grid_spec=grid_spec,
        out_shape=jax.ShapeDtypeStruct((T, W), BF16),
        compiler_params=_cparams(2),
        name="diff_attn",
    )(jnp.asarray(qi_tab), jnp.asarray(ki_tab), q, k, v, bias_tiles, lq1, lk1, lq2, lk2, subln)


ROW_SUB = 16
ROW_LANES = 128


def _split_bf16(x):
    hi = x.astype(BF16)
    lo = (x - hi.astype(F32)).astype(BF16)
    return hi, lo


def _store_row_slabs(dst_ref, val, tm):
    for c in range(ROW_SUB):
        dst_ref[pl.ds(c, tm, stride=ROW_SUB), :] = val[:, ROW_LANES * c:ROW_LANES * (c + 1)]


def _load_row_slabs(src_ref, tm):
    return jnp.concatenate([src_ref[pl.ds(c, tm, stride=ROW_SUB), :] for c in range(ROW_SUB)], axis=-1)


def _proj_out_kernel(om_ref, og_ref, od_ref, x_ref, wo_ref, g_ref, b_ref, rwh_ref, rwl_ref,
                     x1_out, x1s_out, logit_out, *, tm, alpha):
    n_m = om_ref.shape[1]
    n_g = og_ref.shape[1]
    mix = (_dot(om_ref[...], wo_ref[0:n_m, :])
           + _dot(og_ref[...], wo_ref[n_m:n_m + n_g, :])
           + _dot(od_ref[...], wo_ref[n_m + n_g:, :]))
    x1 = _layer_norm(alpha * x_ref[...] + mix, g_ref[...], b_ref[...])
    x1_out[...] = x1
    _store_row_slabs(x1s_out, x1, tm)
    xh, xl = _split_bf16(x1)
    logit_out[...] = _dot_nt(rwh_ref[...], xh) + _dot_nt(rwh_ref[...], xl) + _dot_nt(rwl_ref[...], xh)


def _proj_out(om, og, od, x2, wo, g, b, rwh, rwl, *, tm, alpha):
    T, D = x2.shape
    E = rwh.shape[0]
    row = lambda i: (i, 0)
    return pl.pallas_call(
        functools.partial(_proj_out_kernel, tm=tm, alpha=alpha),
        grid=(T // tm,),
        in_specs=[
            pl.BlockSpec((tm, om.shape[1]), row), pl.BlockSpec((tm, og.shape[1]), row),
            pl.BlockSpec((tm, od.shape[1]), row), pl.BlockSpec((tm, D), row),
            _const_spec(wo.shape), _const_spec(g.shape), _const_spec(b.shape),
            _const_spec(rwh.shape), _const_spec(rwl.shape),
        ],
        out_specs=[
            pl.BlockSpec((tm, D), row),
            pl.BlockSpec((tm * ROW_SUB, ROW_LANES), row),
            pl.BlockSpec((E, tm), lambda i: (0, i)),
        ],
        out_shape=[
            jax.ShapeDtypeStruct((T, D), F32),
            jax.ShapeDtypeStruct((T * ROW_SUB, ROW_LANES), F32),
            jax.ShapeDtypeStruct((E, T), F32),
        ],
        compiler_params=_cparams(1),
        name="proj_out",
    )(om, og, od, x2, wo, g, b, rwh, rwl)


def _first_argmax(v, axis, size):
    mx = jnp.max(v, axis=axis, keepdims=True)
    idx = lax.broadcasted_iota(jnp.int32, v.shape, axis)
    first = jnp.min(jnp.where(v == mx, idx, size), axis=axis, keepdims=True)
    return mx, first


def _route_kernel(logit_ref, bias_ref, gate_out, sel_out):
    E, tn = logit_ref.shape
    per_group = E // N_GROUPS
    scores = jax.nn.sigmoid(logit_ref[...])
    biased = scores + bias_ref[...]
    grouped = biased.reshape(N_GROUPS, per_group, tn)
    member = lax.broadcasted_iota(jnp.int32, grouped.shape, 1)
    top1, first = _first_argmax(grouped, 1, per_group)
    top2 = jnp.max(jnp.where(member == first, NEG_INF, grouped), axis=1, keepdims=True)
    group_score = (top1 + top2).reshape(N_GROUPS, tn)

    group_id = lax.broadcasted_iota(jnp.int32, group_score.shape, 0)
    group_sel = jnp.zeros(group_score.shape, F32)
    remaining = group_score
    for _ in range(TOPK_GROUPS):
        _, first = _first_argmax(remaining, 0, N_GROUPS)
        hit = group_id == first
        group_sel = jnp.where(hit, 1.0, group_sel)
        remaining = jnp.where(hit, NEG_INF, remaining)

    masked = jnp.where(group_sel.reshape(N_GROUPS, 1, tn) > 0.5, grouped, NEG_INF).reshape(E, tn)
    expert_id = lax.broadcasted_iota(jnp.int32, masked.shape, 0)
    sel = jnp.zeros(masked.shape, F32)
    for _ in range(TOP_K):
        _, first = _first_argmax(masked, 0, E)
        hit = expert_id == first
        sel = jnp.where(hit, 1.0, sel)
        masked = jnp.where(hit, NEG_INF, masked)

    w = scores * sel
    gate_out[...] = w / jnp.sum(w, axis=0, keepdims=True) * ROUTED_SCALE
    sel_out[...] = sel


def _route(logits_t, bias_col, *, tn):
    E, T = logits_t.shape
    col = lambda i: (0, i)
    return pl.pallas_call(
        _route_kernel,
        grid=(T // tn,),
        in_specs=[pl.BlockSpec((E, tn), col), _const_spec(bias_col.shape)],
        out_specs=[pl.BlockSpec((E, tn), col), pl.BlockSpec((E, tn), col)],
        out_shape=[jax.ShapeDtypeStruct((E, T), F32), jax.ShapeDtypeStruct((E, T), F32)],
        compiler_params=_cparams(1),
        name="route",
    )(logits_t, bias_col)


def _plan_kernel(sel_ref, gate_ref, start_ref, pos_out, gate_out, carry):
    E, tn = sel_ref.shape

    @pl.when(pl.program_id(0) == 0)
    def _init():
        carry[...] = jnp.zeros(carry.shape, F32)

    sel = sel_ref[...]
    selb = sel.astype(BF16)
    earlier_token = (lax.broadcasted_iota(jnp.int32, (tn, tn), 0) < lax.broadcasted_iota(jnp.int32, (tn, tn), 1))
    rank = _dot(selb, jnp.where(earlier_token, 1.0, 0.0).astype(BF16)) + carry[...]
    smaller_expert = (lax.broadcasted_iota(jnp.int32, (E, E), 1) < lax.broadcasted_iota(jnp.int32, (E, E), 0))
    slot = _dot(jnp.where(smaller_expert, 1.0, 0.0).astype(BF16), selb)
    pos = start_ref[...] + rank
    gates = gate_ref[...]
    for k in range(TOP_K):
        hit = jnp.logical_and(sel > 0.5, slot == float(k))
        pos_out[k:k + 1, :] = jnp.sum(jnp.where(hit, pos, 0.0), axis=0, keepdims=True).astype(jnp.int32)
        gate_out[k:k + 1, :] = jnp.sum(jnp.where(hit, gates, 0.0), axis=0, keepdims=True)
    carry[...] = carry[...] + jnp.sum(sel, axis=1, keepdims=True)


def _plan(sel_t, gates_t, group_start_col, *, tn):
    E, T = sel_t.shape
    col = lambda i: (0, i)
    return pl.pallas_call(
        _plan_kernel,
        grid=(T // tn,),
        in_specs=[pl.BlockSpec((E, tn), col), pl.BlockSpec((E, tn), col), _const_spec(group_start_col.shape)],
        out_specs=[pl.BlockSpec((TOP_K, tn), col), pl.BlockSpec((TOP_K, tn), col)],
        out_shape=[jax.ShapeDtypeStruct((TOP_K, T), jnp.int32), jax.ShapeDtypeStruct((TOP_K, T), F32)],
        scratch_shapes=[pltpu.VMEM((E, 1), F32)],
        compiler_params=_cparams(1),
        name="plan",
    )(sel_t, gates_t, group_start_col)


def _group_layout(sel_t, *, tm, n_tiles):
    E = sel_t.shape[0]
    counts = jnp.sum(sel_t, axis=1).astype(jnp.int32)
    padded = (counts + tm - 1) // tm * tm
    group_end = jnp.cumsum(padded)
    group_start = group_end - padded
    tile_start = jnp.arange(n_tiles, dtype=jnp.int32) * tm
    tile_expert = jnp.sum((tile_start[:, None] >= group_end[None, :]).astype(jnp.int32), axis=1)
    tile_valid = tile_expert < E
    n_valid = jnp.sum(tile_valid.astype(jnp.int32))
    last_expert = jnp.max(jnp.where(counts > 0, jnp.arange(E, dtype=jnp.int32), 0))
    tile_expert = jnp.where(tile_valid, tile_expert, last_expert).astype(jnp.int32)
    tile_block = jnp.minimum(jnp.arange(n_tiles, dtype=jnp.int32), jnp.maximum(n_valid - 1, 0))
    pad_start = (group_start + counts).astype(jnp.int32)
    pad_len = (padded - counts).astype(jnp.int32)
    return (group_start, pad_start, pad_len, n_valid.reshape(1).astype(jnp.int32),
            tile_expert, tile_valid.astype(jnp.int32), tile_block)


DMA_UNROLL = 8


def _dispatch_kernel(pad_ref, padlen_ref, nvalid_ref, pos_ref, x_ref, xs_hbm, zbuf, sems, *, td, tm, n_experts,
                     n_tiles):
    i = pl.program_id(0)

    def zero_fill(wait):
        def piece(row, n_rows):
            copy = pltpu.make_async_copy(
                zbuf.at[pl.ds(0, n_rows * ROW_SUB)],
                xs_hbm.at[pl.ds(pl.multiple_of(row * ROW_SUB, ROW_SUB), n_rows * ROW_SUB)], sems.at[0])
            if wait:
                copy.wait()
            else:
                copy.start()

        def expert_padding(e, carry):
            row = pad_ref[e]
            n = padlen_ref[e]
            size = tm // 2
            while size >= 1:
                pl.when((n & size) != 0)(functools.partial(piece, row, size))
                row = row + (n & size)
                size //= 2
            return carry
        lax.fori_loop(0, n_experts, expert_padding, 0)

        def unused_tile(j, carry):
            piece(j * tm, tm)
            return carry
        lax.fori_loop(nvalid_ref[0], n_tiles, unused_tile, 0)

    @pl.when(i == 0)
    def _start_zero_fill():
        zbuf[...] = jnp.zeros(zbuf.shape, F32)
        zero_fill(wait=False)

    for k in range(TOP_K):
        def copy_rows(j, carry, k=k):
            for u in range(DMA_UNROLL):
                t = j * DMA_UNROLL + u
                pltpu.make_async_copy(
                    x_ref.at[pl.ds(pl.multiple_of(t * ROW_SUB, ROW_SUB), ROW_SUB)],
                    xs_hbm.at[pl.ds(pl.multiple_of(pos_ref[k, t] * ROW_SUB, ROW_SUB), ROW_SUB)],
                    sems.at[1]).start()
            return carry
        lax.fori_loop(0, td // DMA_UNROLL, copy_rows, 0)
    for k in range(TOP_K):
        pltpu.make_async_copy(x_ref, xs_hbm.at[pl.ds(0, td * ROW_SUB)], sems.at[1]).wait()

    @pl.when(i == 0)
    def _finish_zero_fill():
        zero_fill(wait=True)


def _dispatch(pad_start, pad_len, n_valid, pos8, x1s, *, td, tm, n_tiles):
    T = pos8.shape[1]
    E = pad_start.shape[0]
    grid_spec = pltpu.PrefetchScalarGridSpec(
        num_scalar_prefetch=3,
        grid=(T // td,),
        in_specs=[
            pl.BlockSpec((TOP_K, td), lambda i, *_: (0, i), memory_space=pltpu.SMEM),
            pl.BlockSpec((td * ROW_SUB, ROW_LANES), lambda i, *_: (i, 0)),
        ],
        out_specs=pl.BlockSpec(memory_space=pl.ANY),
        scratch_shapes=[pltpu.VMEM((tm * ROW_SUB, ROW_LANES), F32), pltpu.SemaphoreType.DMA((2,))],
    )
    return pl.pallas_call(
        functools.partial(_dispatch_kernel, td=td, tm=tm, n_experts=E, n_tiles=n_tiles),
        grid_spec=grid_spec,
        out_shape=jax.ShapeDtypeStruct((n_tiles * tm * ROW_SUB, ROW_LANES), F32),
        compiler_params=_cparams(1),
        name="dispatch",
    )(pad_start, pad_len, n_valid, pos8, x1s)


def _experts_kernel(te_ref, tv_ref, tb_ref, xs_ref, wg_ref, wu_ref, wd_ref, ys_ref, wgb, wub, wdb, *, tm):
    i = pl.program_id(0)

    @pl.when(tv_ref[i] > 0)
    def _tile():
        @pl.when(jnp.logical_or(i == 0, te_ref[i] != te_ref[jnp.maximum(i - 1, 0)]))
        def _cast_weights():
            wgb[...] = wg_ref[0, 0].astype(BF16)
            wub[...] = wu_ref[0, 0].astype(BF16)
            wdb[...] = wd_ref[0, 0].astype(BF16)

        xb = _load_row_slabs(xs_ref, tm).astype(BF16)
        h = (jax.nn.silu(_dot(xb, wgb[...])) * _dot(xb, wub[...])).astype(BF16)
        _store_row_slabs(ys_ref, _dot(h, wdb[...]), tm)

    @pl.when(tv_ref[i] == 0)
    def _unused_tile():
        ys_ref[...] = jnp.zeros(ys_ref.shape, F32)


def _experts(tile_expert, tile_valid, tile_block, xs, wg, wu, wd, layer, *, tm):
    n_tiles = tile_expert.shape[0]
    D, F = wg.shape[2], wg.shape[3]
    wmap = lambda i, te, tv, tb: (layer, te[i], 0, 0)
    grid_spec = pltpu.PrefetchScalarGridSpec(
        num_scalar_prefetch=3,
        grid=(n_tiles,),
        in_specs=[
            pl.BlockSpec((tm * ROW_SUB, ROW_LANES), lambda i, te, tv, tb: (tb[i], 0)),
            pl.BlockSpec((1, 1, D, F), wmap),
            pl.BlockSpec((1, 1, D, F), wmap),
            pl.BlockSpec((1, 1, F, D), wmap),
        ],
        out_specs=pl.BlockSpec((tm * ROW_SUB, ROW_LANES), lambda i, te, tv, tb: (i, 0)),
        scratch_shapes=[pltpu.VMEM((D, F), BF16), pltpu.VMEM((D, F), BF16), pltpu.VMEM((F, D), BF16)],
    )
    return pl.pallas_call(
        functools.partial(_experts_kernel, tm=tm),
        grid_spec=grid_spec,
        out_shape=jax.ShapeDtypeStruct(xs.shape, F32),
        compiler_params=_cparams(1),
        name="experts",
    )(tile_expert, tile_valid, tile_block, xs, wg, wu, wd)


def _ffn_out_kernel(pos_ref, x_ref, gate_ref, ys_hbm, sg_ref, su_ref, sd_ref, g_ref, b_ref, o_ref,
                    gbuf, sem, *, tc, alpha):
    slab = tc * ROW_SUB
    for k in range(TOP_K):
        def gather_rows(j, carry, k=k):
            for u in range(DMA_UNROLL):
                t = j * DMA_UNROLL + u
                pltpu.make_async_copy(
                    ys_hbm.at[pl.ds(pl.multiple_of(pos_ref[k, t] * ROW_SUB, ROW_SUB), ROW_SUB)],
                    gbuf.at[k, pl.ds(pl.multiple_of(t * ROW_SUB, ROW_SUB), ROW_SUB)],
                    sem.at[0]).start()
            return carry
        lax.fori_loop(0, tc // DMA_UNROLL, gather_rows, 0)

    x1 = x_ref[...]
    xb = x1.astype(BF16)
    hs = (jax.nn.silu(_dot(xb, sg_ref[...])) * _dot(xb, su_ref[...])).astype(BF16)
    ffn = _dot(hs, sd_ref[...])

    gate = gate_ref[...]
    for k in range(TOP_K):
        pltpu.make_async_copy(ys_hbm.at[pl.ds(0, slab)], gbuf.at[k], sem.at[0]).wait()
    for k in range(TOP_K):
        ffn = ffn + gate[:, k:k + 1] * _load_row_slabs(gbuf.at[k], tc)
    o_ref[...] = _layer_norm(alpha * x1 + ffn, g_ref[...], b_ref[...])


def _ffn_out(pos8, x1, gate_tk, ys, sg, su, sd, g, b, *, tc, alpha):
    T, D = x1.shape
    row = lambda i: (i, 0)
    return pl.pallas_call(
        functools.partial(_ffn_out_kernel, tc=tc, alpha=alpha),
        grid=(T // tc,),
        in_specs=[
            pl.BlockSpec((TOP_K, tc), lambda i: (0, i), memory_space=pltpu.SMEM),
            pl.BlockSpec((tc, D), row),
            pl.BlockSpec((tc, TOP_K), row),
            pl.BlockSpec(memory_space=pl.ANY),
            _const_spec(sg.shape), _const_spec(su.shape), _const_spec(sd.shape),
            _const_spec(g.shape), _const_spec(b.shape),
        ],
        out_specs=pl.BlockSpec((tc, D), row),
        out_shape=jax.ShapeDtypeStruct((T, D), F32),
        scratch_shapes=[pltpu.VMEM((TOP_K, tc * ROW_SUB, ROW_LANES), F32), pltpu.SemaphoreType.DMA((1,))],
        compiler_params=_cparams(1),
        name="ffn_out",
    )(pos8, x1, gate_tk, ys, sg, su, sd, g, b)


def _tile_sizes(batch, seq):
    T = batch * seq
    return dict(
        tm=min(256, seq),
        tb=min(512, seq),
        tn=min(512, T),
        td=min(256, T),
        te=min(256, T),
        tc=min(128, T),
    )


def _rope_tables(seq):
    half = MLA_ROPE // 2
    inv = ROPE_THETA ** (-jnp.arange(half, dtype=F32) / half)
    ang = jnp.arange(seq, dtype=jnp.int32).astype(F32)[:, None] * inv[None, :]
    cos, sin = jnp.cos(ang), jnp.sin(ang)
    pad = jnp.zeros((seq, LANES - MLA_ROPE), F32)
    return jnp.concatenate([cos, cos, pad], axis=1), jnp.concatenate([-sin, sin, pad], axis=1)


def _pack_in_weight(w):
    D = w.shape[0]
    kr_end = 512 + 256 + MLA_ROPE
    return jnp.concatenate([w[:, :kr_end], jnp.zeros((D, LANES - MLA_ROPE), w.dtype), w[:, kr_end:]], axis=1).astype(BF16)


def _pack_uq(w):
    r = w.shape[0]
    w3 = w.reshape(r, MLA_HEADS, MLA_NOPE + MLA_ROPE)
    nope = w3[:, :, :MLA_NOPE].reshape(r, MLA_HEADS * MLA_NOPE)
    rope = jnp.pad(w3[:, :, MLA_NOPE:], ((0, 0), (0, 0), (0, LANES - MLA_ROPE))).reshape(r, MLA_HEADS * LANES)
    return jnp.concatenate([nope, rope], axis=1).astype(BF16)


def _pack_ukv(w):
    r = w.shape[0]
    w3 = w.reshape(r, MLA_HEADS, MLA_NOPE + MLA_V)
    return jnp.concatenate([w3[:, :, :MLA_NOPE].reshape(r, -1), w3[:, :, MLA_NOPE:].reshape(r, -1)], axis=1).astype(BF16)


def kernel(x, rel_bias, w_in, mla_q_norm, mla_w_uq, mla_kv_norm, mla_w_ukv, gmlp_ln_g, gmlp_ln_b, gmlp_w_s, gmlp_b_s, diff_lq1, diff_lk1, diff_lq2, diff_lk2, diff_subln, w_o, ln1_g, ln1_b, router_w, router_bias, exp_w_gate, exp_w_up, exp_w_down, shared_w_gate, shared_w_up, shared_w_down, ln2_g, ln2_b):
    B, S, D = x.shape
    T = B * S
    depth = w_in.shape[0]
    E = router_w.shape[-1]
    ts = _tile_sizes(B, S)
    alpha = (2 * depth) ** 0.25
    te = ts["te"]
    n_tiles = (T * TOP_K) // te + E

    cos, sin = _rope_tables(S)
    bias_tiles = _rel_bias_tiles(rel_bias, tb=ts["tb"])
    row2 = lambda a: a.reshape(1, -1)

    x2 = x.reshape(T, D)
    for l in range(depth):
        lambda_init = 0.8 - 0.6 * math.exp(-0.3 * l)
        q, k, v, og, dq, dk, dv = _proj_in(
            x2, cos, sin, _pack_in_weight(w_in[l]), row2(mla_q_norm[l]), _pack_uq(mla_w_uq[l]),
            row2(mla_kv_norm[l]), _pack_ukv(mla_w_ukv[l]), row2(gmlp_ln_g[l]), row2(gmlp_ln_b[l]),
            gmlp_w_s[l], gmlp_b_s[l].T, seq=S, tm=ts["tm"])
        om = _mla_attn(q, k, v, batch=B, seq=S, tb=ts["tb"])
        od = _diff_attn(dq, dk, dv, bias_tiles, row2(diff_lq1[l]), row2(diff_lk1[l]), row2(diff_lq2[l]),
                        row2(diff_lk2[l]), row2(diff_subln[l]), batch=B, seq=S, tb=ts["tb"], lambda_init=lambda_init)
        rwh, rwl = _split_bf16(router_w[l].T)
        x1, x1s, logits_t = _proj_out(om, og, od, x2, w_o[l].astype(BF16), row2(ln1_g[l]), row2(ln1_b[l]),
                                      rwh, rwl, tm=ts["tm"], alpha=alpha)
        gates_t, sel_t = _route(logits_t, router_bias[l].reshape(E, 1), tn=ts["tn"])
        group_start, pad_start, pad_len, n_valid, tile_expert, tile_valid, tile_block = _group_layout(
            sel_t, tm=te, n_tiles=n_tiles)
        pos8, gate8 = _plan(sel_t, gates_t, group_start.astype(F32).reshape(E, 1), tn=ts["tn"])
        xs = _dispatch(pad_start, pad_len, n_valid, pos8, x1s, td=ts["td"], tm=te, n_tiles=n_tiles)
        ys = _experts(tile_expert, tile_valid, tile_block, xs, exp_w_gate, exp_w_up, exp_w_down, l, tm=te)
        x2 = _ffn_out(pos8, x1, gate8.T, ys, shared_w_gate[l].astype(BF16), shared_w_up[l].astype(BF16),
                      shared_w_down[l].astype(BF16), row2(ln2_g[l]), row2(ln2_b[l]), tc=ts["tc"], alpha=alpha)
    return x2.reshape(B, S, D)
```

```python
import functools
import math

import numpy as np
import jax
import jax.numpy as jnp
from jax import lax
from jax.experimental import pallas as pl
from jax.experimental.pallas import tpu as pltpu

F32 = jnp.float32
BF16 = jnp.bfloat16

MLA_HEADS = 8
MLA_NOPE = 128
MLA_ROPE = 64
MLA_V = 128
ROPE_THETA = 10000.0
GMLP_GROUPS = 4
GMLP_CHUNK = 128
DIFF_HEADS = 4
DIFF_QK = 64
DIFF_V = 128
REL_BUCKETS = 32
REL_MAX_EXACT = 16
REL_MAX_DIST = 128
N_GROUPS = 8
TOPK_GROUPS = 4
TOP_K = 8
ROUTED_SCALE = 2.5

LANES = 128
VMEM_LIMIT_BYTES = 56 * 1024 * 1024

NEG_INF = float("-inf")


def _cparams(n_grid_dims):
    return pltpu.CompilerParams(
        dimension_semantics=("arbitrary",) * n_grid_dims,
        vmem_limit_bytes=VMEM_LIMIT_BYTES,
    )


def _const_spec(shape):
    zeros = (0,) * len(shape)
    return pl.BlockSpec(shape, lambda *_: zeros, pipeline_mode=pl.Buffered(1))


def _dot(a, b):
    return jnp.dot(a, b, preferred_element_type=F32)


def _dot_nt(a, b):
    return lax.dot_general(a, b, (((1,), (1,)), ((), ())), preferred_element_type=F32)


def _rms(x, g, eps=1e-6):
    ms = jnp.mean(jnp.square(x), axis=-1, keepdims=True)
    return x * lax.rsqrt(ms + eps) * g


def _layer_norm(x, g, b, eps=1e-5):
    mu = jnp.mean(x, axis=-1, keepdims=True)
    xc = x - mu
    var = jnp.mean(jnp.square(xc), axis=-1, keepdims=True)
    return xc * lax.rsqrt(var + eps) * g + b


def _swap_rope_halves(x):
    n = x.shape[-1]
    lane = lax.broadcasted_iota(jnp.int32, x.shape, x.ndim - 1) % LANES
    half = MLA_ROPE // 2
    return jnp.where(lane < half, pltpu.roll(x, n - half, x.ndim - 1), pltpu.roll(x, half, x.ndim - 1))


_C_Q = (0, 512)
_C_KV = (512, 768)
_C_KR = (768, 896)
_C_GU = (896, 1408)
_C_GV = (1408, 1920)
_C_DQ = (1920, 2432)
_C_DK = (2432, 2944)

LOG2_E = math.log2(math.e)


def _proj_in_kernel(x_ref, cos_ref, sin_ref, wa_ref, wdvt_ref, qg_ref, wuq_ref, kvg_ref, wuk_ref, wuvt_ref,
                    lng_ref, lnb_ref, ws_ref, bst_ref,
                    q_out, k_out, vt_out, og_out, dq_out, dk_out, dvt_out, *, tm):
    xb = x_ref[...].astype(BF16)

    def proj(cols):
        return _dot(xb, wa_ref[:, cols[0]:cols[1]])

    cos = cos_ref[...]
    sin = sin_ref[...]
    mla_scale = (MLA_NOPE + MLA_ROPE) ** -0.5 * LOG2_E
    nope_w = MLA_HEADS * MLA_NOPE

    q = _dot(_rms(proj(_C_Q), qg_ref[...]).astype(BF16), wuq_ref[...])
    q_rope = q[:, nope_w:]
    cos8 = jnp.concatenate([cos] * MLA_HEADS, axis=-1)
    sin8 = jnp.concatenate([sin] * MLA_HEADS, axis=-1)
    q_rope = q_rope * cos8 + _swap_rope_halves(q_rope) * sin8
    for h in range(MLA_HEADS):
        q_out[:, 2 * LANES * h:2 * LANES * h + LANES] = (q[:, LANES * h:LANES * (h + 1)] * mla_scale).astype(BF16)
        q_out[:, 2 * LANES * h + LANES:2 * LANES * (h + 1)] = (
            q_rope[:, LANES * h:LANES * (h + 1)] * mla_scale).astype(BF16)

    kvn = _rms(proj(_C_KV), kvg_ref[...]).astype(BF16)
    k_nope = _dot(kvn, wuk_ref[...])
    k_rope = proj(_C_KR)
    k_rope = (k_rope * cos + _swap_rope_halves(k_rope) * sin).astype(BF16)
    for h in range(MLA_HEADS):
        k_out[:, 2 * LANES * h:2 * LANES * h + LANES] = k_nope[:, LANES * h:LANES * (h + 1)].astype(BF16)
        k_out[:, 2 * LANES * h + LANES:2 * LANES * (h + 1)] = k_rope
    vt_out[...] = _dot_nt(wuvt_ref[...], kvn).astype(BF16)

    u = jax.nn.gelu(proj(_C_GU))
    v = _layer_norm(jax.nn.gelu(proj(_C_GV)), lng_ref[...], lnb_ref[...])
    row = lax.broadcasted_iota(jnp.int32, (GMLP_CHUNK, GMLP_CHUNK), 0)
    col = lax.broadcasted_iota(jnp.int32, (GMLP_CHUNK, GMLP_CHUNK), 1)
    causal = col <= row
    for g in range(GMLP_GROUPS):
        ws = jnp.where(causal, ws_ref[g], 0.0).astype(BF16)
        bias = bst_ref[:, g:g + 1]
        for c in range(tm // GMLP_CHUNK):
            rows = slice(c * GMLP_CHUNK, (c + 1) * GMLP_CHUNK)
            cols = slice(g * LANES, (g + 1) * LANES)
            mixed = _dot(ws, v[rows, cols].astype(BF16)) + bias
            og_out[rows, cols] = (u[rows, cols] * mixed).astype(BF16)

    dq_out[...] = (proj(_C_DQ) * (DIFF_QK ** -0.5 * LOG2_E)).astype(BF16)
    dk_out[...] = proj(_C_DK).astype(BF16)
    dvt_out[...] = _dot_nt(wdvt_ref[...], xb).astype(BF16)


def _proj_in(x2, cos, sin, wa, wdvt, qg, wuq, kvg, wuk, wuvt, lng, lnb, ws, bst, *, seq, tm):
    T, D = x2.shape
    n_pos_blocks = seq // tm
    row = lambda i: (i, 0)
    col = lambda i: (0, i)
    pos = lambda i: (i % n_pos_blocks, 0)
    outs = ((2 * LANES * MLA_HEADS, False), (2 * LANES * MLA_HEADS, False), (MLA_V * MLA_HEADS, True),
            (GMLP_GROUPS * LANES, False), (DIFF_HEADS * LANES, False), (DIFF_HEADS * LANES, False),
            (DIFF_HEADS * DIFF_V, True))
    consts = (wa, wdvt, qg, wuq, kvg, wuk, wuvt, lng, lnb, ws, bst)
    return pl.pallas_call(
        functools.partial(_proj_in_kernel, tm=tm),
        grid=(T // tm,),
        in_specs=[
            pl.BlockSpec((tm, D), row),
            pl.BlockSpec((tm, LANES), pos),
            pl.BlockSpec((tm, LANES), pos),
        ] + [_const_spec(a.shape) for a in consts],
        out_specs=[pl.BlockSpec((w, tm), col) if t else pl.BlockSpec((tm, w), row) for w, t in outs],
        out_shape=[jax.ShapeDtypeStruct((w, T) if t else (T, w), BF16) for w, t in outs],
        compiler_params=_cparams(1),
        name="proj_in",
    )(x2, cos, sin, *consts)


def _causal_pairs(n_blocks):
    qi, ki = [], []
    for q in range(n_blocks):
        for k in range(q + 1):
            qi.append(q)
            ki.append(k)
    return np.asarray(qi, np.int32), np.asarray(ki, np.int32)


def _online_softmax_step(st, vt, m_ref, l_ref, acc_ref):
    m_prev = m_ref[...]
    m_new = jnp.maximum(m_prev, jnp.max(st, axis=0, keepdims=True))
    alpha = jnp.exp2(m_prev - m_new)
    p = jnp.exp2(st - m_new)
    l_ref[...] = alpha * l_ref[...] + jnp.sum(p, axis=0, keepdims=True)
    acc_ref[...] = alpha * acc_ref[...] + _dot(vt, p.astype(BF16))
    m_ref[...] = m_new


def _causal_keep_t(tb):
    key = lax.broadcasted_iota(jnp.int32, (tb, tb), 0)
    query = lax.broadcasted_iota(jnp.int32, (tb, tb), 1)
    return key <= query


def _mla_attn_kernel(qi_ref, ki_ref, q_ref, k_ref, vt_ref, o_ref, m_scr, l_scr, acc_scr, *, tb):
    p = pl.program_id(1)
    qi = qi_ref[p]
    ki = ki_ref[p]

    @pl.when(ki == 0)
    def _init():
        m_scr[...] = jnp.full(m_scr.shape, NEG_INF, F32)
        l_scr[...] = jnp.zeros(l_scr.shape, F32)
        acc_scr[...] = jnp.zeros(acc_scr.shape, F32)

    def step(on_diagonal):
        if on_diagonal:
            keep = _causal_keep_t(tb)
        for h in range(MLA_HEADS):
            qk = slice(2 * LANES * h, 2 * LANES * (h + 1))
            st = _dot_nt(k_ref[:, qk], q_ref[:, qk])
            if on_diagonal:
                st = jnp.where(keep, st, NEG_INF)
            _online_softmax_step(st, vt_ref[MLA_V * h:MLA_V * (h + 1), :], m_scr.at[h], l_scr.at[h], acc_scr.at[h])

    @pl.when(ki != qi)
    def _below():
        step(False)

    @pl.when(ki == qi)
    def _diag():
        step(True)
        for h in range(MLA_HEADS):
            o_ref[:, MLA_V * h:MLA_V * (h + 1)] = (acc_scr[h] / l_scr[h]).T.astype(BF16)


def _mla_attn(q, k, vt, *, batch, seq, tb):
    T = q.shape[0]
    W = vt.shape[0]
    nb = seq // tb
    qi_tab, ki_tab = _causal_pairs(nb)
    qmap = lambda b, p, qi, ki: (b * nb + qi[p], 0)
    kmap = lambda b, p, qi, ki: (b * nb + ki[p], 0)
    vmap = lambda b, p, qi, ki: (0, b * nb + ki[p])
    grid_spec = pltpu.PrefetchScalarGridSpec(
        num_scalar_prefetch=2,
        grid=(batch, len(qi_tab)),
        in_specs=[
            pl.BlockSpec((tb, q.shape[1]), qmap),
            pl.BlockSpec((tb, k.shape[1]), kmap),
            pl.BlockSpec((W, tb), vmap),
        ],
        out_specs=pl.BlockSpec((tb, W), qmap),
        scratch_shapes=[
            pltpu.VMEM((MLA_HEADS, 1, tb), F32),
            pltpu.VMEM((MLA_HEADS, 1, tb), F32),
            pltpu.VMEM((MLA_HEADS, MLA_V, tb), F32),
        ],
    )
    return pl.pallas_call(
        functools.partial(_mla_attn_kernel, tb=tb),
        grid_spec=grid_spec,
        out_shape=jax.ShapeDtypeStruct((T, W), BF16),
        compiler_params=_cparams(2),
        name="mla_attn",
    )(jnp.asarray(qi_tab), jnp.asarray(ki_tab), q, k, vt)


def _t5_causal_bucket(n):
    is_small = n < REL_MAX_EXACT
    nf = jnp.maximum(n, 1).astype(F32)
    large = REL_MAX_EXACT + (jnp.log(nf / REL_MAX_EXACT) / math.log(REL_MAX_DIST / REL_MAX_EXACT)
                             * (REL_BUCKETS - REL_MAX_EXACT)).astype(jnp.int32)
    large = jnp.minimum(large, REL_BUCKETS - 1)
    return jnp.where(is_small, n, large)


def _rel_bias_kernel(tab_ref, o_ref, *, tb):
    slab = pl.program_id(0)
    key = lax.broadcasted_iota(jnp.int32, (tb, tb), 0)
    query = lax.broadcasted_iota(jnp.int32, (tb, tb), 1)
    bucket = _t5_causal_bucket(jnp.maximum(slab * tb + query - key, 0))
    for h in range(DIFF_HEADS):
        far = tab_ref[REL_BUCKETS - 1, h]
        acc = jnp.zeros((tb, tb), F32)
        for b in range(REL_BUCKETS - 1):
            acc = jnp.where(bucket == b, (tab_ref[b, h] - far) * LOG2_E, acc)
        o_ref[h, 0] = acc


def _rel_bias_tiles(rel_bias, *, tb):
    return pl.pallas_call(
        functools.partial(_rel_bias_kernel, tb=tb),
        grid=(2,),
        in_specs=[pl.BlockSpec(memory_space=pltpu.SMEM)],
        out_specs=pl.BlockSpec((DIFF_HEADS, 1, tb, tb), lambda s: (0, s, 0, 0)),
        out_shape=jax.ShapeDtypeStruct((DIFF_HEADS, 2, tb, tb), F32),
        compiler_params=_cparams(1),
        name="rel_bias",
    )(rel_bias)


def _diff_attn_kernel(qi_ref, ki_ref, q_ref, k_ref, vt_ref, bias_ref, lq1_ref, lk1_ref, lq2_ref, lk2_ref, g_ref,
                      o_ref, m_scr, l_scr, acc_scr, *, tb, lambda_init):
    p = pl.program_id(1)
    qi = qi_ref[p]
    ki = ki_ref[p]

    @pl.when(ki == 0)
    def _init():
        m_scr[...] = jnp.full(m_scr.shape, NEG_INF, F32)
        l_scr[...] = jnp.zeros(l_scr.shape, F32)
        acc_scr[...] = jnp.zeros(acc_scr.shape, F32)

    def step(slab):
        if slab == 0:
            keep = _causal_keep_t(tb)
        lane = lax.broadcasted_iota(jnp.int32, (tb, LANES), 1)
        for h in range(DIFF_HEADS):
            hs = slice(LANES * h, LANES * (h + 1))
            qh = q_ref[:, hs]
            kh = k_ref[:, hs]
            vth = vt_ref[DIFF_V * h:DIFF_V * (h + 1), :]
            for m in range(2):
                in_map = (lane < DIFF_QK) if m == 0 else (lane >= DIFF_QK)
                st = _dot_nt(kh, jnp.where(in_map, qh, jnp.zeros_like(qh)))
                if slab is not None:
                    st = st + bias_ref[h, slab]
                if slab == 0:
                    st = jnp.where(keep, st, NEG_INF)
                i = 2 * h + m
                _online_softmax_step(st, vth, m_scr.at[i], l_scr.at[i], acc_scr.at[i])

    @pl.when(ki < qi - 1)
    def _far():
        step(None)

    @pl.when(ki == qi - 1)
    def _near():
        step(1)

    @pl.when(ki == qi)
    def _diag():
        step(0)
        lam = (jnp.exp(jnp.sum(lq1_ref[...] * lk1_ref[...], axis=-1, keepdims=True))
               - jnp.exp(jnp.sum(lq2_ref[...] * lk2_ref[...], axis=-1, keepdims=True)) + lambda_init)
        for h in range(DIFF_HEADS):
            ot = acc_scr[2 * h] / l_scr[2 * h] - lam * (acc_scr[2 * h + 1] / l_scr[2 * h + 1])
            ms = jnp.mean(jnp.square(ot), axis=0, keepdims=True)
            ot = ot * lax.rsqrt(ms + 1e-6) * g_ref[...] * (1.0 - lambda_init)
            o_ref[:, DIFF_V * h:DIFF_V * (h + 1)] = ot.T.astype(BF16)


def _diff_attn(q, k, vt, bias_tiles, lq1, lk1, lq2, lk2, subln_col, *, batch, seq, tb, lambda_init):
    T, W = q.shape
    nb = seq // tb
    qi_tab, ki_tab = _causal_pairs(nb)
    qmap = lambda b, p, qi, ki: (b * nb + qi[p], 0)
    kmap = lambda b, p, qi, ki: (b * nb + ki[p], 0)
    vmap = lambda b, p, qi, ki: (0, b * nb + ki[p])
    small = lambda a: _const_spec(a.shape)
    grid_spec = pltpu.PrefetchScalarGridSpec(
        num_scalar_prefetch=2,
        grid=(batch, len(qi_tab)),
        in_specs=[
            pl.BlockSpec((tb, W), qmap),
            pl.BlockSpec((tb, W), kmap),
            pl.BlockSpec((vt.shape[0], tb), vmap),
            small(bias_tiles), small(lq1), small(lk1), small(lq2), small(lk2), small(subln_col),
        ],
        out_specs=pl.BlockSpec((tb, vt.shape[0]), qmap),
        scratch_shapes=[
            pltpu.VMEM((2 * DIFF_HEADS, 1, tb), F32),
            pltpu.VMEM((2 * DIFF_HEADS, 1, tb), F32),
            pltpu.VMEM((2 * DIFF_HEADS, DIFF_V, tb), F32),
        ],
    )
    return pl.pallas_call(
        functools.partial(_diff_attn_kernel, tb=tb, lambda_init=lambda_init),
        grid_spec=grid_spec,
        out_shape=jax.ShapeDtypeStruct((T, vt.shape[0]), BF16),
        compiler_params=_cparams(2),
        name="diff_attn",
    )(jnp.asarray(qi_tab), jnp.asarray(ki_tab), q, k, vt, bias_tiles, lq1, lk1, lq2, lk2, subln_col)


ROW_SUB = 16
ROW_LANES = 128


def _split_bf16(x):
    hi = x.astype(BF16)
    lo = (x - hi.astype(F32)).astype(BF16)
    return hi, lo


def _store_row_slabs(dst_ref, val, tm):
    for c in range(ROW_SUB):
        dst_ref[pl.ds(c, tm, stride=ROW_SUB), :] = val[:, ROW_LANES * c:ROW_LANES * (c + 1)]


def _load_row_slabs(src_ref, tm):
    return jnp.concatenate([src_ref[pl.ds(c, tm, stride=ROW_SUB), :] for c in range(ROW_SUB)], axis=-1)


def _proj_out_kernel(om_ref, og_ref, od_ref, x_ref, wo_ref, g_ref, b_ref, rwh_ref, rwl_ref,
                     x1_out, x1s_out, logit_out, *, tm, alpha):
    n_m = om_ref.shape[1]
    n_g = og_ref.shape[1]
    mix = (_dot(om_ref[...], wo_ref[0:n_m, :])
           + _dot(og_ref[...], wo_ref[n_m:n_m + n_g, :])
           + _dot(od_ref[...], wo_ref[n_m + n_g:, :]))
    x1 = _layer_norm(alpha * x_ref[...] + mix, g_ref[...], b_ref[...])
    x1_out[...] = x1
    _store_row_slabs(x1s_out, x1, tm)
    xh, xl = _split_bf16(x1)
    logit_out[...] = _dot_nt(rwh_ref[...], xh) + _dot_nt(rwh_ref[...], xl) + _dot_nt(rwl_ref[...], xh)


def _proj_out(om, og, od, x2, wo, g, b, rwh, rwl, *, tm, alpha):
    T, D = x2.shape
    E = rwh.shape[0]
    row = lambda i: (i, 0)
    return pl.pallas_call(
        functools.partial(_proj_out_kernel, tm=tm, alpha=alpha),
        grid=(T // tm,),
        in_specs=[
            pl.BlockSpec((tm, om.shape[1]), row), pl.BlockSpec((tm, og.shape[1]), row),
            pl.BlockSpec((tm, od.shape[1]), row), pl.BlockSpec((tm, D), row),
            _const_spec(wo.shape), _const_spec(g.shape), _const_spec(b.shape),
            _const_spec(rwh.shape), _const_spec(rwl.shape),
        ],
        out_specs=[
            pl.BlockSpec((tm, D), row),
            pl.BlockSpec((tm * ROW_SUB, ROW_LANES), row),
            pl.BlockSpec((E, tm), lambda i: (0, i)),
        ],
        out_shape=[
            jax.ShapeDtypeStruct((T, D), F32),
            jax.ShapeDtypeStruct((T * ROW_SUB, ROW_LANES), F32),
            jax.ShapeDtypeStruct((E, T), F32),
        ],
        compiler_params=_cparams(1),
        name="proj_out",
    )(om, og, od, x2, wo, g, b, rwh, rwl)


def _first_argmax(v, axis, size):
    mx = jnp.max(v, axis=axis, keepdims=True)
    idx = lax.broadcasted_iota(jnp.int32, v.shape, axis)
    first = jnp.min(jnp.where(v == mx, idx, size), axis=axis, keepdims=True)
    return mx, first


def _route_kernel(logit_ref, bias_ref, gate_out, sel_out):
    E, tn = logit_ref.shape
    per_group = E // N_GROUPS
    scores = jax.nn.sigmoid(logit_ref[...])
    biased = scores + bias_ref[...]
    grouped = biased.reshape(N_GROUPS, per_group, tn)
    member = lax.broadcasted_iota(jnp.int32, grouped.shape, 1)
    top1, first = _first_argmax(grouped, 1, per_group)
    top2 = jnp.max(jnp.where(member == first, NEG_INF, grouped), axis=1, keepdims=True)
    group_score = (top1 + top2).reshape(N_GROUPS, tn)

    group_id = lax.broadcasted_iota(jnp.int32, group_score.shape, 0)
    group_sel = jnp.zeros(group_score.shape, F32)
    remaining = group_score
    for _ in range(TOPK_GROUPS):
        _, first = _first_argmax(remaining, 0, N_GROUPS)
        hit = group_id == first
        group_sel = jnp.where(hit, 1.0, group_sel)
        remaining = jnp.where(hit, NEG_INF, remaining)

    masked = jnp.where(group_sel.reshape(N_GROUPS, 1, tn) > 0.5, grouped, NEG_INF).reshape(E, tn)
    expert_id = lax.broadcasted_iota(jnp.int32, masked.shape, 0)
    sel = jnp.zeros(masked.shape, F32)
    for _ in range(TOP_K):
        _, first = _first_argmax(masked, 0, E)
        hit = expert_id == first
        sel = jnp.where(hit, 1.0, sel)
        masked = jnp.where(hit, NEG_INF, masked)

    w = scores * sel
    gate_out[...] = w / jnp.sum(w, axis=0, keepdims=True) * ROUTED_SCALE
    sel_out[...] = sel


def _route(logits_t, bias_col, *, tn):
    E, T = logits_t.shape
    col = lambda i: (0, i)
    return pl.pallas_call(
        _route_kernel,
        grid=(T // tn,),
        in_specs=[pl.BlockSpec((E, tn), col), _const_spec(bias_col.shape)],
        out_specs=[pl.BlockSpec((E, tn), col), pl.BlockSpec((E, tn), col)],
        out_shape=[jax.ShapeDtypeStruct((E, T), F32), jax.ShapeDtypeStruct((E, T), F32)],
        compiler_params=_cparams(1),
        name="route",
    )(logits_t, bias_col)


def _plan_kernel(sel_ref, gate_ref, start_ref, pos_out, gate_out, carry):
    E, tn = sel_ref.shape

    @pl.when(pl.program_id(0) == 0)
    def _init():
        carry[...] = jnp.zeros(carry.shape, F32)

    sel = sel_ref[...]
    selb = sel.astype(BF16)
    earlier_token = (lax.broadcasted_iota(jnp.int32, (tn, tn), 0) < lax.broadcasted_iota(jnp.int32, (tn, tn), 1))
    rank = _dot(selb, jnp.where(earlier_token, 1.0, 0.0).astype(BF16)) + carry[...]
    smaller_expert = (lax.broadcasted_iota(jnp.int32, (E, E), 1) < lax.broadcasted_iota(jnp.int32, (E, E), 0))
    slot = _dot(jnp.where(smaller_expert, 1.0, 0.0).astype(BF16), selb)
    pos = start_ref[...] + rank
    gates = gate_ref[...]
    for k in range(TOP_K):
        hit = jnp.logical_and(sel > 0.5, slot == float(k))
        pos_out[k:k + 1, :] = jnp.sum(jnp.where(hit, pos, 0.0), axis=0, keepdims=True).astype(jnp.int32)
        gate_out[k:k + 1, :] = jnp.sum(jnp.where(hit, gates, 0.0), axis=0, keepdims=True)
    carry[...] = carry[...] + jnp.sum(sel, axis=1, keepdims=True)


def _plan(sel_t, gates_t, group_start_col, *, tn):
    E, T = sel_t.shape
    col = lambda i: (0, i)
    return pl.pallas_call(
        _plan_kernel,
        grid=(T // tn,),
        in_specs=[pl.BlockSpec((E, tn), col), pl.BlockSpec((E, tn), col), _const_spec(group_start_col.shape)],
        out_specs=[pl.BlockSpec((TOP_K, tn), col), pl.BlockSpec((TOP_K, tn), col)],
        out_shape=[jax.ShapeDtypeStruct((TOP_K, T), jnp.int32), jax.ShapeDtypeStruct((TOP_K, T), F32)],
        scratch_shapes=[pltpu.VMEM((E, 1), F32)],
        compiler_params=_cparams(1),
        name="plan",
    )(sel_t, gates_t, group_start_col)


def _group_layout(sel_t, *, tm, n_tiles):
    E = sel_t.shape[0]
    counts = jnp.sum(sel_t, axis=1).astype(jnp.int32)
    padded = (counts + tm - 1) // tm * tm
    group_end = jnp.cumsum(padded)
    group_start = group_end - padded
    tile_start = jnp.arange(n_tiles, dtype=jnp.int32) * tm
    tile_expert = jnp.sum((tile_start[:, None] >= group_end[None, :]).astype(jnp.int32), axis=1)
    tile_valid = tile_expert < E
    n_valid = jnp.sum(tile_valid.astype(jnp.int32))
    last_expert = jnp.max(jnp.where(counts > 0, jnp.arange(E, dtype=jnp.int32), 0))
    tile_expert = jnp.where(tile_valid, tile_expert, last_expert).astype(jnp.int32)
    tile_block = jnp.minimum(jnp.arange(n_tiles, dtype=jnp.int32), jnp.maximum(n_valid - 1, 0))
    pad_start = (group_start + counts).astype(jnp.int32)
    pad_len = (padded - counts).astype(jnp.int32)
    return (group_start, pad_start, pad_len, n_valid.reshape(1).astype(jnp.int32),
            tile_expert, tile_valid.astype(jnp.int32), tile_block)


DMA_UNROLL = 8


def _dispatch_kernel(pad_ref, padlen_ref, nvalid_ref, pos_ref, x_ref, xs_hbm, zbuf, sems, *, td, tm, n_experts,
                     n_tiles):
    i = pl.program_id(0)

    def zero_fill(wait):
        def piece(row, n_rows):
            copy = pltpu.make_async_copy(
                zbuf.at[pl.ds(0, n_rows * ROW_SUB)],
                xs_hbm.at[pl.ds(pl.multiple_of(row * ROW_SUB, ROW_SUB), n_rows * ROW_SUB)], sems.at[0])
            if wait:
                copy.wait()
            else:
                copy.start()

        def expert_padding(e, carry):
            row = pad_ref[e]
            n = padlen_ref[e]
            size = tm // 2
            while size >= 1:
                pl.when((n & size) != 0)(functools.partial(piece, row, size))
                row = row + (n & size)
                size //= 2
            return carry
        lax.fori_loop(0, n_experts, expert_padding, 0)

        def unused_tile(j, carry):
            piece(j * tm, tm)
            return carry
        lax.fori_loop(nvalid_ref[0], n_tiles, unused_tile, 0)

    @pl.when(i == 0)
    def _start_zero_fill():
        zbuf[...] = jnp.zeros(zbuf.shape, F32)
        zero_fill(wait=False)

    for k in range(TOP_K):
        def copy_rows(j, carry, k=k):
            for u in range(DMA_UNROLL):
                t = j * DMA_UNROLL + u
                pltpu.make_async_copy(
                    x_ref.at[pl.ds(pl.multiple_of(t * ROW_SUB, ROW_SUB), ROW_SUB)],
                    xs_hbm.at[pl.ds(pl.multiple_of(pos_ref[k, t] * ROW_SUB, ROW_SUB), ROW_SUB)],
                    sems.at[1]).start()
            return carry
        lax.fori_loop(0, td // DMA_UNROLL, copy_rows, 0)
    for k in range(TOP_K):
        pltpu.make_async_copy(x_ref, xs_hbm.at[pl.ds(0, td * ROW_SUB)], sems.at[1]).wait()

    @pl.when(i == 0)
    def _finish_zero_fill():
        zero_fill(wait=True)


def _dispatch(pad_start, pad_len, n_valid, pos8, x1s, *, td, tm, n_tiles):
    T = pos8.shape[1]
    E = pad_start.shape[0]
    grid_spec = pltpu.PrefetchScalarGridSpec(
        num_scalar_prefetch=3,
        grid=(T // td,),
        in_specs=[
            pl.BlockSpec((TOP_K, td), lambda i, *_: (0, i), memory_space=pltpu.SMEM),
            pl.BlockSpec((td * ROW_SUB, ROW_LANES), lambda i, *_: (i, 0)),
        ],
        out_specs=pl.BlockSpec(memory_space=pl.ANY),
        scratch_shapes=[pltpu.VMEM((tm * ROW_SUB, ROW_LANES), F32), pltpu.SemaphoreType.DMA((2,))],
    )
    return pl.pallas_call(
        functools.partial(_dispatch_kernel, td=td, tm=tm, n_experts=E, n_tiles=n_tiles),
        grid_spec=grid_spec,
        out_shape=jax.ShapeDtypeStruct((n_tiles * tm * ROW_SUB, ROW_LANES), F32),
        compiler_params=_cparams(1),
        name="dispatch",
    )(pad_start, pad_len, n_valid, pos8, x1s)


def _experts_kernel(te_ref, tv_ref, tb_ref, xs_ref, wg_ref, wu_ref, wd_ref, ys_ref, wgb, wub, wdb, *, tm):
    i = pl.program_id(0)

    @pl.when(tv_ref[i] > 0)
    def _tile():
        @pl.when(jnp.logical_or(i == 0, te_ref[i] != te_ref[jnp.maximum(i - 1, 0)]))
        def _cast_weights():
            wgb[...] = wg_ref[0, 0].astype(BF16)
            wub[...] = wu_ref[0, 0].astype(BF16)
            wdb[...] = wd_ref[0, 0].astype(BF16)

        xb = _load_row_slabs(xs_ref, tm).astype(BF16)
        h = (jax.nn.silu(_dot(xb, wgb[...])) * _dot(xb, wub[...])).astype(BF16)
        _store_row_slabs(ys_ref, _dot(h, wdb[...]), tm)

    @pl.when(tv_ref[i] == 0)
    def _unused_tile():
        ys_ref[...] = jnp.zeros(ys_ref.shape, F32)


def _experts(tile_expert, tile_valid, tile_block, xs, wg, wu, wd, layer, *, tm):
    n_tiles = tile_expert.shape[0]
    D, F = wg.shape[2], wg.shape[3]
    wmap = lambda i, te, tv, tb: (layer, te[i], 0, 0)
    grid_spec = pltpu.PrefetchScalarGridSpec(
        num_scalar_prefetch=3,
        grid=(n_tiles,),
        in_specs=[
            pl.BlockSpec((tm * ROW_SUB, ROW_LANES), lambda i, te, tv, tb: (tb[i], 0)),
            pl.BlockSpec((1, 1, D, F), wmap),
            pl.BlockSpec((1, 1, D, F), wmap),
            pl.BlockSpec((1, 1, F, D), wmap),
        ],
        out_specs=pl.BlockSpec((tm * ROW_SUB, ROW_LANES), lambda i, te, tv, tb: (i, 0)),
        scratch_shapes=[pltpu.VMEM((D, F), BF16), pltpu.VMEM((D, F), BF16), pltpu.VMEM((F, D), BF16)],
    )
    return pl.pallas_call(
        functools.partial(_experts_kernel, tm=tm),
        grid_spec=grid_spec,
        out_shape=jax.ShapeDtypeStruct(xs.shape, F32),
        compiler_params=_cparams(1),
        name="experts",
    )(tile_expert, tile_valid, tile_block, xs, wg, wu, wd)


def _ffn_out_kernel(pos_ref, x_ref, gate_ref, ys_hbm, sg_ref, su_ref, sd_ref, g_ref, b_ref, o_ref,
                    gbuf, sem, *, tc, alpha):
    slab = tc * ROW_SUB
    for k in range(TOP_K):
        def gather_rows(j, carry, k=k):
            for u in range(DMA_UNROLL):
                t = j * DMA_UNROLL + u
                pltpu.make_async_copy(
                    ys_hbm.at[pl.ds(pl.multiple_of(pos_ref[k, t] * ROW_SUB, ROW_SUB), ROW_SUB)],
                    gbuf.at[k, pl.ds(pl.multiple_of(t * ROW_SUB, ROW_SUB), ROW_SUB)],
                    sem.at[0]).start()
            return carry
        lax.fori_loop(0, tc // DMA_UNROLL, gather_rows, 0)

    x1 = x_ref[...]
    xb = x1.astype(BF16)
    hs = (jax.nn.silu(_dot(xb, sg_ref[...])) * _dot(xb, su_ref[...])).astype(BF16)
    ffn = _dot(hs, sd_ref[...])

    gate = gate_ref[...]
    for k in range(TOP_K):
        pltpu.make_async_copy(ys_hbm.at[pl.ds(0, slab)], gbuf.at[k], sem.at[0]).wait()
    for k in range(TOP_K):
        ffn = ffn + gate[:, k:k + 1] * _load_row_slabs(gbuf.at[k], tc)
    o_ref[...] = _layer_norm(alpha * x1 + ffn, g_ref[...], b_ref[...])


def _ffn_out(pos8, x1, gate_tk, ys, sg, su, sd, g, b, *, tc, alpha):
    T, D = x1.shape
    row = lambda i: (i, 0)
    return pl.pallas_call(
        functools.partial(_ffn_out_kernel, tc=tc, alpha=alpha),
        grid=(T // tc,),
        in_specs=[
            pl.BlockSpec((TOP_K, tc), lambda i: (0, i), memory_space=pltpu.SMEM),
            pl.BlockSpec((tc, D), row),
            pl.BlockSpec((tc, TOP_K), row),
            pl.BlockSpec(memory_space=pl.ANY),
            _const_spec(sg.shape), _const_spec(su.shape), _const_spec(sd.shape),
            _const_spec(g.shape), _const_spec(b.shape),
        ],
        out_specs=pl.BlockSpec((tc, D), row),
        out_shape=jax.ShapeDtypeStruct((T, D), F32),
        scratch_shapes=[pltpu.VMEM((TOP_K, tc * ROW_SUB, ROW_LANES), F32), pltpu.SemaphoreType.DMA((1,))],
        compiler_params=_cparams(1),
        name="ffn_out",
    )(pos8, x1, gate_tk, ys, sg, su, sd, g, b)


def _tile_sizes(batch, seq):
    T = batch * seq
    return dict(
        tm=min(256, seq),
        tb=min(512, seq),
        tn=min(512, T),
        td=min(256, T),
        te=min(256, T),
        tc=min(128, T),
    )


def _rope_tables(seq):
    half = MLA_ROPE // 2
    inv = ROPE_THETA ** (-jnp.arange(half, dtype=F32) / half)
    ang = jnp.arange(seq, dtype=jnp.int32).astype(F32)[:, None] * inv[None, :]
    cos, sin = jnp.cos(ang), jnp.sin(ang)
    pad = jnp.zeros((seq, LANES - MLA_ROPE), F32)
    return jnp.concatenate([cos, cos, pad], axis=1), jnp.concatenate([-sin, sin, pad], axis=1)


def _pack_in_weight(w):
    D = w.shape[0]
    kr_end = 512 + 256 + MLA_ROPE
    dv_start = w.shape[1] - DIFF_HEADS * DIFF_V
    wa = jnp.concatenate([w[:, :kr_end], jnp.zeros((D, LANES - MLA_ROPE), w.dtype), w[:, kr_end:dv_start]], axis=1)
    return wa.astype(BF16), w[:, dv_start:].T.astype(BF16)


def _pack_uq(w):
    r = w.shape[0]
    w3 = w.reshape(r, MLA_HEADS, MLA_NOPE + MLA_ROPE)
    nope = w3[:, :, :MLA_NOPE].reshape(r, MLA_HEADS * MLA_NOPE)
    rope = jnp.pad(w3[:, :, MLA_NOPE:], ((0, 0), (0, 0), (0, LANES - MLA_ROPE))).reshape(r, MLA_HEADS * LANES)
    return jnp.concatenate([nope, rope], axis=1).astype(BF16)


def _pack_ukv(w):
    r = w.shape[0]
    w3 = w.reshape(r, MLA_HEADS, MLA_NOPE + MLA_V)
    return w3[:, :, :MLA_NOPE].reshape(r, -1).astype(BF16), w3[:, :, MLA_NOPE:].reshape(r, -1).T.astype(BF16)


def kernel(x, rel_bias, w_in, mla_q_norm, mla_w_uq, mla_kv_norm, mla_w_ukv, gmlp_ln_g, gmlp_ln_b, gmlp_w_s, gmlp_b_s, diff_lq1, diff_lk1, diff_lq2, diff_lk2, diff_subln, w_o, ln1_g, ln1_b, router_w, router_bias, exp_w_gate, exp_w_up, exp_w_down, shared_w_gate, shared_w_up, shared_w_down, ln2_g, ln2_b):
    B, S, D = x.shape
    T = B * S
    depth = w_in.shape[0]
    E = router_w.shape[-1]
    ts = _tile_sizes(B, S)
    alpha = (2 * depth) ** 0.25
    te = ts["te"]
    n_tiles = (T * TOP_K) // te + E

    cos, sin = _rope_tables(S)
    bias_tiles = _rel_bias_tiles(rel_bias, tb=ts["tb"])
    row2 = lambda a: a.reshape(1, -1)

    x2 = x.reshape(T, D)
    for l in range(depth):
        lambda_init = 0.8 - 0.6 * math.exp(-0.3 * l)
        wa, wdvt = _pack_in_weight(w_in[l])
        wuk, wuvt = _pack_ukv(mla_w_ukv[l])
        q, k, vt, og, dq, dk, dvt = _proj_in(
            x2, cos, sin, wa, wdvt, row2(mla_q_norm[l]), _pack_uq(mla_w_uq[l]),
            row2(mla_kv_norm[l]), wuk, wuvt, row2(gmlp_ln_g[l]), row2(gmlp_ln_b[l]),
            gmlp_w_s[l], gmlp_b_s[l].T, seq=S, tm=ts["tm"])
        om = _mla_attn(q, k, vt, batch=B, seq=S, tb=ts["tb"])
        od = _diff_attn(dq, dk, dvt, bias_tiles, row2(diff_lq1[l]), row2(diff_lk1[l]), row2(diff_lq2[l]),
                        row2(diff_lk2[l]), diff_subln[l].reshape(-1, 1), batch=B, seq=S, tb=ts["tb"],
                        lambda_init=lambda_init)
        rwh, rwl = _split_bf16(router_w[l].T)
        x1, x1s, logits_t = _proj_out(om, og, od, x2, w_o[l].astype(BF16), row2(ln1_g[l]), row2(ln1_b[l]),
                                      rwh, rwl, tm=ts["tm"], alpha=alpha)
        gates_t, sel_t = _route(logits_t, router_bias[l].reshape(E, 1), tn=ts["tn"])
        group_start, pad_start, pad_len, n_valid, tile_expert, tile_valid, tile_block = _group_layout(
            sel_t, tm=te, n_tiles=n_tiles)
        pos8, gate8 = _plan(sel_t, gates_t, group_start.astype(F32).reshape(E, 1), tn=ts["tn"])
        xs = _dispatch(pad_start, pad_len, n_valid, pos8, x1s, td=ts["td"], tm=te, n_tiles=n_tiles)
        ys = _experts(tile_expert, tile_valid, tile_block, xs, exp_w_gate, exp_w_up, exp_w_down, l, tm=te)
        x2 = _ffn_out(pos8, x1, gate8.T, ys, shared_w_gate[l].astype(BF16), shared_w_up[l].astype(BF16),
                      shared_w_down[l].astype(BF16), row2(ln2_g[l]), row2(ln2_b[l]), tc=ts["tc"], alpha=alpha)
    return x2.reshape(B, S, D)
```

```python
import functools
import math

import numpy as np
import jax
import jax.numpy as jnp
from jax import lax
from jax.experimental import pallas as pl
from jax.experimental.pallas import tpu as pltpu

F32 = jnp.float32
BF16 = jnp.bfloat16

MLA_HEADS = 8
MLA_NOPE = 128
MLA_ROPE = 64
MLA_V = 128
ROPE_THETA = 10000.0
GMLP_GROUPS = 4
GMLP_CHUNK = 128
DIFF_HEADS = 4
DIFF_QK = 64
DIFF_V = 128
REL_BUCKETS = 32
REL_MAX_EXACT = 16
REL_MAX_DIST = 128
N_GROUPS = 8
TOPK_GROUPS = 4
TOP_K = 8
ROUTED_SCALE = 2.5

LANES = 128
VMEM_LIMIT_BYTES = 56 * 1024 * 1024

NEG_INF = float("-inf")


def _cparams(n_grid_dims):
    return pltpu.CompilerParams(
        dimension_semantics=("arbitrary",) * n_grid_dims,
        vmem_limit_bytes=VMEM_LIMIT_BYTES,
    )


def _const_spec(shape):
    zeros = (0,) * len(shape)
    return pl.BlockSpec(shape, lambda *_: zeros, pipeline_mode=pl.Buffered(1))


def _dot(a, b):
    return jnp.dot(a, b, preferred_element_type=F32)


def _dot_nt(a, b):
    return lax.dot_general(a, b, (((1,), (1,)), ((), ())), preferred_element_type=F32)


def _rms(x, g, eps=1e-6):
    ms = jnp.mean(jnp.square(x), axis=-1, keepdims=True)
    return x * lax.rsqrt(ms + eps) * g


def _layer_norm(x, g, b, eps=1e-5):
    mu = jnp.mean(x, axis=-1, keepdims=True)
    xc = x - mu
    var = jnp.mean(jnp.square(xc), axis=-1, keepdims=True)
    return xc * lax.rsqrt(var + eps) * g + b


def _swap_rope_halves(x):
    n = x.shape[-1]
    lane = lax.broadcasted_iota(jnp.int32, x.shape, x.ndim - 1) % LANES
    half = MLA_ROPE // 2
    return jnp.where(lane < half, pltpu.roll(x, n - half, x.ndim - 1), pltpu.roll(x, half, x.ndim - 1))


_C_Q = (0, 512)
_C_KV = (512, 768)
_C_KR = (768, 896)
_C_GU = (896, 1408)
_C_GV = (1408, 1920)
_C_DQ = (1920, 2432)
_C_DK = (2432, 2944)

LOG2_E = math.log2(math.e)


def _proj_in_kernel(x_ref, cos_ref, sin_ref, wa_ref, wdvt_ref, qg_ref, wuq_ref, kvg_ref, wuk_ref, wuvt_ref,
                    lng_ref, lnb_ref, ws_ref, bst_ref,
                    q_out, k_out, vt_out, og_out, dq_out, dk_out, dvt_out, *, tm):
    xb = x_ref[...].astype(BF16)

    def proj(cols):
        return _dot(xb, wa_ref[:, cols[0]:cols[1]])

    cos = cos_ref[...]
    sin = sin_ref[...]
    mla_scale = (MLA_NOPE + MLA_ROPE) ** -0.5 * LOG2_E
    nope_w = MLA_HEADS * MLA_NOPE

    q = _dot(_rms(proj(_C_Q), qg_ref[...]).astype(BF16), wuq_ref[...])
    q_rope = q[:, nope_w:]
    cos8 = jnp.concatenate([cos] * MLA_HEADS, axis=-1)
    sin8 = jnp.concatenate([sin] * MLA_HEADS, axis=-1)
    q_rope = q_rope * cos8 + _swap_rope_halves(q_rope) * sin8
    for h in range(MLA_HEADS):
        q_out[:, 2 * LANES * h:2 * LANES * h + LANES] = (q[:, LANES * h:LANES * (h + 1)] * mla_scale).astype(BF16)
        q_out[:, 2 * LANES * h + LANES:2 * LANES * (h + 1)] = (
            q_rope[:, LANES * h:LANES * (h + 1)] * mla_scale).astype(BF16)

    kvn = _rms(proj(_C_KV), kvg_ref[...]).astype(BF16)
    k_nope = _dot(kvn, wuk_ref[...])
    k_rope = proj(_C_KR)
    k_rope = (k_rope * cos + _swap_rope_halves(k_rope) * sin).astype(BF16)
    for h in range(MLA_HEADS):
        k_out[:, 2 * LANES * h:2 * LANES * h + LANES] = k_nope[:, LANES * h:LANES * (h + 1)].astype(BF16)
        k_out[:, 2 * LANES * h + LANES:2 * LANES * (h + 1)] = k_rope
    vt_out[...] = _dot_nt(wuvt_ref[...], kvn).astype(BF16)

    u = jax.nn.gelu(proj(_C_GU))
    v = _layer_norm(jax.nn.gelu(proj(_C_GV)), lng_ref[...], lnb_ref[...])
    row = lax.broadcasted_iota(jnp.int32, (GMLP_CHUNK, GMLP_CHUNK), 0)
    col = lax.broadcasted_iota(jnp.int32, (GMLP_CHUNK, GMLP_CHUNK), 1)
    causal = col <= row
    for g in range(GMLP_GROUPS):
        ws = jnp.where(causal, ws_ref[g], 0.0).astype(BF16)
        bias = bst_ref[:, g:g + 1]
        for c in range(tm // GMLP_CHUNK):
            rows = slice(c * GMLP_CHUNK, (c + 1) * GMLP_CHUNK)
            cols = slice(g * LANES, (g + 1) * LANES)
            mixed = _dot(ws, v[rows, cols].astype(BF16)) + bias
            og_out[rows, cols] = (u[rows, cols] * mixed).astype(BF16)

    dq_out[...] = (proj(_C_DQ) * (DIFF_QK ** -0.5 * LOG2_E)).astype(BF16)
    dk_out[...] = proj(_C_DK).astype(BF16)
    dvt_out[...] = _dot_nt(wdvt_ref[...], xb).astype(BF16)


def _proj_in(x2, cos, sin, wa, wdvt, qg, wuq, kvg, wuk, wuvt, lng, lnb, ws, bst, *, seq, tm):
    T, D = x2.shape
    n_pos_blocks = seq // tm
    row = lambda i: (i, 0)
    col = lambda i: (0, i)
    pos = lambda i: (i % n_pos_blocks, 0)
    outs = ((2 * LANES * MLA_HEADS, False), (2 * LANES * MLA_HEADS, False), (MLA_V * MLA_HEADS, True),
            (GMLP_GROUPS * LANES, False), (DIFF_HEADS * LANES, False), (DIFF_HEADS * LANES, False),
            (DIFF_HEADS * DIFF_V, True))
    consts = (wa, wdvt, qg, wuq, kvg, wuk, wuvt, lng, lnb, ws, bst)
    return pl.pallas_call(
        functools.partial(_proj_in_kernel, tm=tm),
        grid=(T // tm,),
        in_specs=[
            pl.BlockSpec((tm, D), row),
            pl.BlockSpec((tm, LANES), pos),
            pl.BlockSpec((tm, LANES), pos),
        ] + [_const_spec(a.shape) for a in consts],
        out_specs=[pl.BlockSpec((w, tm), col) if t else pl.BlockSpec((tm, w), row) for w, t in outs],
        out_shape=[jax.ShapeDtypeStruct((w, T) if t else (T, w), BF16) for w, t in outs],
        compiler_params=_cparams(1),
        name="proj_in",
    )(x2, cos, sin, *consts)


def _causal_pairs(n_blocks):
    qi, ki = [], []
    for q in range(n_blocks):
        for k in range(q + 1):
            qi.append(q)
            ki.append(k)
    return np.asarray(qi, np.int32), np.asarray(ki, np.int32)


def _online_softmax_step(st, vt, m_ref, l_ref, acc_ref):
    m_prev = m_ref[...]
    m_new = jnp.maximum(m_prev, jnp.max(st, axis=0, keepdims=True))
    alpha = jnp.exp2(m_prev - m_new)
    p = jnp.exp2(st - m_new)
    l_ref[...] = alpha * l_ref[...] + jnp.sum(p, axis=0, keepdims=True)
    acc_ref[...] = alpha * acc_ref[...] + _dot(vt, p.astype(BF16))
    m_ref[...] = m_new


def _causal_keep_t(tb):
    key = lax.broadcasted_iota(jnp.int32, (tb, tb), 0)
    query = lax.broadcasted_iota(jnp.int32, (tb, tb), 1)
    return key <= query


def _mla_attn_kernel(qi_ref, ki_ref, q_ref, k_ref, vt_ref, o_ref, m_scr, l_scr, acc_scr, *, tb):
    p = pl.program_id(1)
    qi = qi_ref[p]
    ki = ki_ref[p]

    @pl.when(ki == 0)
    def _init():
        m_scr[...] = jnp.full(m_scr.shape, NEG_INF, F32)
        l_scr[...] = jnp.zeros(l_scr.shape, F32)
        acc_scr[...] = jnp.zeros(acc_scr.shape, F32)

    def step(on_diagonal):
        if on_diagonal:
            keep = _causal_keep_t(tb)
        for h in range(MLA_HEADS):
            qk = slice(2 * LANES * h, 2 * LANES * (h + 1))
            st = _dot_nt(k_ref[:, qk], q_ref[:, qk])
            if on_diagonal:
                st = jnp.where(keep, st, NEG_INF)
            _online_softmax_step(st, vt_ref[MLA_V * h:MLA_V * (h + 1), :], m_scr.at[h], l_scr.at[h], acc_scr.at[h])

    @pl.when(ki != qi)
    def _below():
        step(False)

    @pl.when(ki == qi)
    def _diag():
        step(True)
        for h in range(MLA_HEADS):
            o_ref[:, MLA_V * h:MLA_V * (h + 1)] = (acc_scr[h] / l_scr[h]).T.astype(BF16)


def _mla_attn(q, k, vt, *, batch, seq, tb):
    T = q.shape[0]
    W = vt.shape[0]
    nb = seq // tb
    qi_tab, ki_tab = _causal_pairs(nb)
    qmap = lambda b, p, qi, ki: (b * nb + qi[p], 0)
    kmap = lambda b, p, qi, ki: (b * nb + ki[p], 0)
    vmap = lambda b, p, qi, ki: (0, b * nb + ki[p])
    grid_spec = pltpu.PrefetchScalarGridSpec(
        num_scalar_prefetch=2,
        grid=(batch, len(qi_tab)),
        in_specs=[
            pl.BlockSpec((tb, q.shape[1]), qmap),
            pl.BlockSpec((tb, k.shape[1]), kmap),
            pl.BlockSpec((W, tb), vmap),
        ],
        out_specs=pl.BlockSpec((tb, W), qmap),
        scratch_shapes=[
            pltpu.VMEM((MLA_HEADS, 1, tb), F32),
            pltpu.VMEM((MLA_HEADS, 1, tb), F32),
            pltpu.VMEM((MLA_HEADS, MLA_V, tb), F32),
        ],
    )
    return pl.pallas_call(
        functools.partial(_mla_attn_kernel, tb=tb),
        grid_spec=grid_spec,
        out_shape=jax.ShapeDtypeStruct((T, W), BF16),
        compiler_params=_cparams(2),
        name="mla_attn",
    )(jnp.asarray(qi_tab), jnp.asarray(ki_tab), q, k, vt)


def _t5_causal_bucket(n):
    is_small = n < REL_MAX_EXACT
    nf = jnp.maximum(n, 1).astype(F32)
    large = REL_MAX_EXACT + (jnp.log(nf / REL_MAX_EXACT) / math.log(REL_MAX_DIST / REL_MAX_EXACT)
                             * (REL_BUCKETS - REL_MAX_EXACT)).astype(jnp.int32)
    large = jnp.minimum(large, REL_BUCKETS - 1)
    return jnp.where(is_small, n, large)


def _rel_bias_kernel(tab_ref, o_ref, *, tb):
    slab = pl.program_id(0)
    key = lax.broadcasted_iota(jnp.int32, (tb, tb), 0)
    query = lax.broadcasted_iota(jnp.int32, (tb, tb), 1)
    bucket = _t5_causal_bucket(jnp.maximum(slab * tb + query - key, 0))
    for h in range(DIFF_HEADS):
        far = tab_ref[REL_BUCKETS - 1, h]
        acc = jnp.zeros((tb, tb), F32)
        for b in range(REL_BUCKETS - 1):
            acc = jnp.where(bucket == b, (tab_ref[b, h] - far) * LOG2_E, acc)
        o_ref[h, 0] = acc


def _rel_bias_tiles(rel_bias, *, tb):
    return pl.pallas_call(
        functools.partial(_rel_bias_kernel, tb=tb),
        grid=(2,),
        in_specs=[pl.BlockSpec(memory_space=pltpu.SMEM)],
        out_specs=pl.BlockSpec((DIFF_HEADS, 1, tb, tb), lambda s: (0, s, 0, 0)),
        out_shape=jax.ShapeDtypeStruct((DIFF_HEADS, 2, tb, tb), F32),
        compiler_params=_cparams(1),
        name="rel_bias",
    )(rel_bias)


def _diff_attn_kernel(qi_ref, ki_ref, q_ref, k_ref, vt_ref, bias_ref, lq1_ref, lk1_ref, lq2_ref, lk2_ref, g_ref,
                      o_ref, m_scr, l_scr, acc_scr, *, tb, lambda_init):
    p = pl.program_id(1)
    qi = qi_ref[p]
    ki = ki_ref[p]

    @pl.when(ki == 0)
    def _init():
        m_scr[...] = jnp.full(m_scr.shape, NEG_INF, F32)
        l_scr[...] = jnp.zeros(l_scr.shape, F32)
        acc_scr[...] = jnp.zeros(acc_scr.shape, F32)

    def step(slab):
        if slab == 0:
            keep = _causal_keep_t(tb)
        lane = lax.broadcasted_iota(jnp.int32, (tb, LANES), 1)
        for h in range(DIFF_HEADS):
            hs = slice(LANES * h, LANES * (h + 1))
            qh = q_ref[:, hs]
            kh = k_ref[:, hs]
            vth = vt_ref[DIFF_V * h:DIFF_V * (h + 1), :]
            for m in range(2):
                in_map = (lane < DIFF_QK) if m == 0 else (lane >= DIFF_QK)
                st = _dot_nt(kh, jnp.where(in_map, qh, jnp.zeros_like(qh)))
                if slab is not None:
                    st = st + bias_ref[h, slab]
                if slab == 0:
                    st = jnp.where(keep, st, NEG_INF)
                i = 2 * h + m
                _online_softmax_step(st, vth, m_scr.at[i], l_scr.at[i], acc_scr.at[i])

    @pl.when(ki < qi - 1)
    def _far():
        step(None)

    @pl.when(ki == qi - 1)
    def _near():
        step(1)

    @pl.when(ki == qi)
    def _diag():
        step(0)
        lam = (jnp.exp(jnp.sum(lq1_ref[...] * lk1_ref[...], axis=-1, keepdims=True))
               - jnp.exp(jnp.sum(lq2_ref[...] * lk2_ref[...], axis=-1, keepdims=True)) + lambda_init)
        for h in range(DIFF_HEADS):
            ot = acc_scr[2 * h] / l_scr[2 * h] - lam * (acc_scr[2 * h + 1] / l_scr[2 * h + 1])
            ms = jnp.mean(jnp.square(ot), axis=0, keepdims=True)
            ot = ot * lax.rsqrt(ms + 1e-6) * g_ref[...] * (1.0 - lambda_init)
            o_ref[:, DIFF_V * h:DIFF_V * (h + 1)] = ot.T.astype(BF16)


def _diff_attn(q, k, vt, bias_tiles, lq1, lk1, lq2, lk2, subln_col, *, batch, seq, tb, lambda_init):
    T, W = q.shape
    nb = seq // tb
    qi_tab, ki_tab = _causal_pairs(nb)
    qmap = lambda b, p, qi, ki: (b * nb + qi[p], 0)
    kmap = lambda b, p, qi, ki: (b * nb + ki[p], 0)
    vmap = lambda b, p, qi, ki: (0, b * nb + ki[p])
    small = lambda a: _const_spec(a.shape)
    grid_spec = pltpu.PrefetchScalarGridSpec(
        num_scalar_prefetch=2,
        grid=(batch, len(qi_tab)),
        in_specs=[
            pl.BlockSpec((tb, W), qmap),
            pl.BlockSpec((tb, W), kmap),
            pl.BlockSpec((vt.shape[0], tb), vmap),
            small(bias_tiles), small(lq1), small(lk1), small(lq2), small(lk2), small(subln_col),
        ],
        out_specs=pl.BlockSpec((tb, vt.shape[0]), qmap),
        scratch_shapes=[
            pltpu.VMEM((2 * DIFF_HEADS, 1, tb), F32),
            pltpu.VMEM((2 * DIFF_HEADS, 1, tb), F32),
            pltpu.VMEM((2 * DIFF_HEADS, DIFF_V, tb), F32),
        ],
    )
    return pl.pallas_call(
        functools.partial(_diff_attn_kernel, tb=tb, lambda_init=lambda_init),
        grid_spec=grid_spec,
        out_shape=jax.ShapeDtypeStruct((T, vt.shape[0]), BF16),
        compiler_params=_cparams(2),
        name="diff_attn",
    )(jnp.asarray(qi_tab), jnp.asarray(ki_tab), q, k, vt, bias_tiles, lq1, lk1, lq2, lk2, subln_col)


ROW_SUB = 8
ROW_LANES = 128
ROW_DTYPE = jnp.int32
_HIGH_HALF = -65536


def _split_bf16(x):
    hi = x.astype(BF16)
    lo = (x - hi.astype(F32)).astype(BF16)
    return hi, lo


def _f32_bits(x):
    return lax.bitcast_convert_type(x, jnp.int32)


def _store_row_slabs(dst_ref, val, tm):
    half = ROW_SUB * ROW_LANES
    for c in range(ROW_SUB):
        lo = _f32_bits(val[:, ROW_LANES * c:ROW_LANES * (c + 1)].astype(BF16).astype(F32))
        hi = _f32_bits(val[:, half + ROW_LANES * c:half + ROW_LANES * (c + 1)].astype(BF16).astype(F32))
        word = lax.shift_right_logical(lo, jnp.full_like(lo, 16)) | (hi & _HIGH_HALF)
        dst_ref[pl.ds(c, tm, stride=ROW_SUB), :] = word


def _load_row_slabs(src_ref, tm):
    lows, highs = [], []
    for c in range(ROW_SUB):
        word = src_ref[pl.ds(c, tm, stride=ROW_SUB), :]
        lows.append(lax.bitcast_convert_type(word << 16, F32))
        highs.append(lax.bitcast_convert_type(word & _HIGH_HALF, F32))
    return jnp.concatenate(lows + highs, axis=-1)


def _proj_out_kernel(om_ref, og_ref, od_ref, x_ref, wo_ref, g_ref, b_ref, rwh_ref, rwl_ref,
                     x1_out, x1s_out, logit_out, *, tm, alpha):
    n_m = om_ref.shape[1]
    n_g = og_ref.shape[1]
    mix = (_dot(om_ref[...], wo_ref[0:n_m, :])
           + _dot(og_ref[...], wo_ref[n_m:n_m + n_g, :])
           + _dot(od_ref[...], wo_ref[n_m + n_g:, :]))
    x1 = _layer_norm(alpha * x_ref[...] + mix, g_ref[...], b_ref[...])
    x1_out[...] = x1
    _store_row_slabs(x1s_out, x1, tm)
    xh, xl = _split_bf16(x1)
    logit_out[...] = _dot_nt(rwh_ref[...], xh) + _dot_nt(rwh_ref[...], xl) + _dot_nt(rwl_ref[...], xh)


def _proj_out(om, og, od, x2, wo, g, b, rwh, rwl, *, tm, alpha):
    T, D = x2.shape
    E = rwh.shape[0]
    row = lambda i: (i, 0)
    return pl.pallas_call(
        functools.partial(_proj_out_kernel, tm=tm, alpha=alpha),
        grid=(T // tm,),
        in_specs=[
            pl.BlockSpec((tm, om.shape[1]), row), pl.BlockSpec((tm, og.shape[1]), row),
            pl.BlockSpec((tm, od.shape[1]), row), pl.BlockSpec((tm, D), row),
            _const_spec(wo.shape), _const_spec(g.shape), _const_spec(b.shape),
            _const_spec(rwh.shape), _const_spec(rwl.shape),
        ],
        out_specs=[
            pl.BlockSpec((tm, D), row),
            pl.BlockSpec((tm * ROW_SUB, ROW_LANES), row),
            pl.BlockSpec((E, tm), lambda i: (0, i)),
        ],
        out_shape=[
            jax.ShapeDtypeStruct((T, D), F32),
            jax.ShapeDtypeStruct((T * ROW_SUB, ROW_LANES), ROW_DTYPE),
            jax.ShapeDtypeStruct((E, T), F32),
        ],
        compiler_params=_cparams(1),
        name="proj_out",
    )(om, og, od, x2, wo, g, b, rwh, rwl)


def _first_argmax(v, axis, size):
    mx = jnp.max(v, axis=axis, keepdims=True)
    idx = lax.broadcasted_iota(jnp.int32, v.shape, axis)
    first = jnp.min(jnp.where(v == mx, idx, size), axis=axis, keepdims=True)
    return mx, first


def _route_kernel(logit_ref, bias_ref, gate_out, sel_out):
    E, tn = logit_ref.shape
    per_group = E // N_GROUPS
    scores = jax.nn.sigmoid(logit_ref[...])
    biased = scores + bias_ref[...]
    grouped = biased.reshape(N_GROUPS, per_group, tn)
    member = lax.broadcasted_iota(jnp.int32, grouped.shape, 1)
    top1, first = _first_argmax(grouped, 1, per_group)
    top2 = jnp.max(jnp.where(member == first, NEG_INF, grouped), axis=1, keepdims=True)
    group_score = (top1 + top2).reshape(N_GROUPS, tn)

    group_id = lax.broadcasted_iota(jnp.int32, group_score.shape, 0)
    group_sel = jnp.zeros(group_score.shape, F32)
    remaining = group_score
    for _ in range(TOPK_GROUPS):
        _, first = _first_argmax(remaining, 0, N_GROUPS)
        hit = group_id == first
        group_sel = jnp.where(hit, 1.0, group_sel)
        remaining = jnp.where(hit, NEG_INF, remaining)

    masked = jnp.where(group_sel.reshape(N_GROUPS, 1, tn) > 0.5, grouped, NEG_INF).reshape(E, tn)
    expert_id = lax.broadcasted_iota(jnp.int32, masked.shape, 0)
    sel = jnp.zeros(masked.shape, F32)
    for _ in range(TOP_K):
        _, first = _first_argmax(masked, 0, E)
        hit = expert_id == first
        sel = jnp.where(hit, 1.0, sel)
        masked = jnp.where(hit, NEG_INF, masked)

    w = scores * sel
    gate_out[...] = w / jnp.sum(w, axis=0, keepdims=True) * ROUTED_SCALE
    sel_out[...] = sel


def _route(logits_t, bias_col, *, tn):
    E, T = logits_t.shape
    col = lambda i: (0, i)
    return pl.pallas_call(
        _route_kernel,
        grid=(T // tn,),
        in_specs=[pl.BlockSpec((E, tn), col), _const_spec(bias_col.shape)],
        out_specs=[pl.BlockSpec((E, tn), col), pl.BlockSpec((E, tn), col)],
        out_shape=[jax.ShapeDtypeStruct((E, T), F32), jax.ShapeDtypeStruct((E, T), F32)],
        compiler_params=_cparams(1),
        name="route",
    )(logits_t, bias_col)


def _plan_kernel(sel_ref, gate_ref, start_ref, pos_out, gate_out, carry):
    E, tn = sel_ref.shape

    @pl.when(pl.program_id(0) == 0)
    def _init():
        carry[...] = jnp.zeros(carry.shape, F32)

    sel = sel_ref[...]
    selb = sel.astype(BF16)
    earlier_token = (lax.broadcasted_iota(jnp.int32, (tn, tn), 0) < lax.broadcasted_iota(jnp.int32, (tn, tn), 1))
    rank = _dot(selb, jnp.where(earlier_token, 1.0, 0.0).astype(BF16)) + carry[...]
    smaller_expert = (lax.broadcasted_iota(jnp.int32, (E, E), 1) < lax.broadcasted_iota(jnp.int32, (E, E), 0))
    slot = _dot(jnp.where(smaller_expert, 1.0, 0.0).astype(BF16), selb)
    pos = start_ref[...] + rank
    gates = gate_ref[...]
    for k in range(TOP_K):
        hit = jnp.logical_and(sel > 0.5, slot == float(k))
        pos_out[k:k + 1, :] = jnp.sum(jnp.where(hit, pos, 0.0), axis=0, keepdims=True).astype(jnp.int32)
        gate_out[k:k + 1, :] = jnp.sum(jnp.where(hit, gates, 0.0), axis=0, keepdims=True)
    carry[...] = carry[...] + jnp.sum(sel, axis=1, keepdims=True)


def _plan(sel_t, gates_t, group_start_col, *, tn):
    E, T = sel_t.shape
    col = lambda i: (0, i)
    return pl.pallas_call(
        _plan_kernel,
        grid=(T // tn,),
        in_specs=[pl.BlockSpec((E, tn), col), pl.BlockSpec((E, tn), col), _const_spec(group_start_col.shape)],
        out_specs=[pl.BlockSpec((TOP_K, tn), col), pl.BlockSpec((TOP_K, tn), col)],
        out_shape=[jax.ShapeDtypeStruct((TOP_K, T), jnp.int32), jax.ShapeDtypeStruct((TOP_K, T), F32)],
        scratch_shapes=[pltpu.VMEM((E, 1), F32)],
        compiler_params=_cparams(1),
        name="plan",
    )(sel_t, gates_t, group_start_col)


def _group_layout(sel_t, *, tm, n_tiles):
    E = sel_t.shape[0]
    counts = jnp.sum(sel_t, axis=1).astype(jnp.int32)
    padded = (counts + tm - 1) // tm * tm
    group_end = jnp.cumsum(padded)
    group_start = group_end - padded
    tile_start = jnp.arange(n_tiles, dtype=jnp.int32) * tm
    tile_expert = jnp.sum((tile_start[:, None] >= group_end[None, :]).astype(jnp.int32), axis=1)
    tile_valid = tile_expert < E
    n_valid = jnp.sum(tile_valid.astype(jnp.int32))
    last_expert = jnp.max(jnp.where(counts > 0, jnp.arange(E, dtype=jnp.int32), 0))
    tile_expert = jnp.where(tile_valid, tile_expert, last_expert).astype(jnp.int32)
    tile_block = jnp.minimum(jnp.arange(n_tiles, dtype=jnp.int32), jnp.maximum(n_valid - 1, 0))
    pad_start = (group_start + counts).astype(jnp.int32)
    pad_len = (padded - counts).astype(jnp.int32)
    return (group_start, pad_start, pad_len, n_valid.reshape(1).astype(jnp.int32),
            tile_expert, tile_valid.astype(jnp.int32), tile_block)


DMA_UNROLL = 8


def _dispatch_kernel(pad_ref, padlen_ref, nvalid_ref, pos_ref, x_ref, xs_hbm, zbuf, sems, *, td, tm, n_experts,
                     n_tiles):
    i = pl.program_id(0)

    def zero_fill(wait):
        def piece(row, n_rows):
            copy = pltpu.make_async_copy(
                zbuf.at[pl.ds(0, n_rows * ROW_SUB)],
                xs_hbm.at[pl.ds(pl.multiple_of(row * ROW_SUB, ROW_SUB), n_rows * ROW_SUB)], sems.at[0])
            if wait:
                copy.wait()
            else:
                copy.start()

        def expert_padding(e, carry):
            row = pad_ref[e]
            n = padlen_ref[e]
            size = tm // 2
            while size >= 1:
                pl.when((n & size) != 0)(functools.partial(piece, row, size))
                row = row + (n & size)
                size //= 2
            return carry
        lax.fori_loop(0, n_experts, expert_padding, 0)

        def unused_tile(j, carry):
            piece(j * tm, tm)
            return carry
        lax.fori_loop(nvalid_ref[0], n_tiles, unused_tile, 0)

    @pl.when(i == 0)
    def _start_zero_fill():
        zbuf[...] = jnp.zeros(zbuf.shape, ROW_DTYPE)
        zero_fill(wait=False)

    for k in range(TOP_K):
        def copy_rows(j, carry, k=k):
            for u in range(DMA_UNROLL):
                t = j * DMA_UNROLL + u
                pltpu.make_async_copy(
                    x_ref.at[pl.ds(pl.multiple_of(t * ROW_SUB, ROW_SUB), ROW_SUB)],
                    xs_hbm.at[pl.ds(pl.multiple_of(pos_ref[k, t] * ROW_SUB, ROW_SUB), ROW_SUB)],
                    sems.at[1]).start(priority=u % 2)
            return carry
        lax.fori_loop(0, td // DMA_UNROLL, copy_rows, 0)
    for k in range(TOP_K):
        pltpu.make_async_copy(x_ref, xs_hbm.at[pl.ds(0, td * ROW_SUB)], sems.at[1]).wait()

    @pl.when(i == 0)
    def _finish_zero_fill():
        zero_fill(wait=True)


def _dispatch(pad_start, pad_len, n_valid, pos8, x1s, *, td, tm, n_tiles):
    T = pos8.shape[1]
    E = pad_start.shape[0]
    grid_spec = pltpu.PrefetchScalarGridSpec(
        num_scalar_prefetch=3,
        grid=(T // td,),
        in_specs=[
            pl.BlockSpec((TOP_K, td), lambda i, *_: (0, i), memory_space=pltpu.SMEM),
            pl.BlockSpec((td * ROW_SUB, ROW_LANES), lambda i, *_: (i, 0)),
        ],
        out_specs=pl.BlockSpec(memory_space=pl.ANY),
        scratch_shapes=[pltpu.VMEM((tm * ROW_SUB, ROW_LANES), ROW_DTYPE), pltpu.SemaphoreType.DMA((2,))],
    )
    return pl.pallas_call(
        functools.partial(_dispatch_kernel, td=td, tm=tm, n_experts=E, n_tiles=n_tiles),
        grid_spec=grid_spec,
        out_shape=jax.ShapeDtypeStruct((n_tiles * tm * ROW_SUB, ROW_LANES), ROW_DTYPE),
        compiler_params=_cparams(1),
        name="dispatch",
    )(pad_start, pad_len, n_valid, pos8, x1s)


def _experts_kernel(te_ref, tv_ref, tb_ref, xs_ref, wg_ref, wu_ref, wd_ref, ys_ref, wgb, wub, wdb, *, tm):
    i = pl.program_id(0)

    @pl.when(tv_ref[i] > 0)
    def _tile():
        @pl.when(jnp.logical_or(i == 0, te_ref[i] != te_ref[jnp.maximum(i - 1, 0)]))
        def _cast_weights():
            wgb[...] = wg_ref[0, 0].astype(BF16)
            wub[...] = wu_ref[0, 0].astype(BF16)
            wdb[...] = wd_ref[0, 0].astype(BF16)

        xb = _load_row_slabs(xs_ref, tm).astype(BF16)
        h = (jax.nn.silu(_dot(xb, wgb[...])) * _dot(xb, wub[...])).astype(BF16)
        _store_row_slabs(ys_ref, _dot(h, wdb[...]), tm)

    @pl.when(tv_ref[i] == 0)
    def _unused_tile():
        ys_ref[...] = jnp.zeros(ys_ref.shape, ROW_DTYPE)


def _experts(tile_expert, tile_valid, tile_block, xs, wg, wu, wd, layer, *, tm):
    n_tiles = tile_expert.shape[0]
    D, F = wg.shape[2], wg.shape[3]
    wmap = lambda i, te, tv, tb: (layer, te[i], 0, 0)
    grid_spec = pltpu.PrefetchScalarGridSpec(
        num_scalar_prefetch=3,
        grid=(n_tiles,),
        in_specs=[
            pl.BlockSpec((tm * ROW_SUB, ROW_LANES), lambda i, te, tv, tb: (tb[i], 0)),
            pl.BlockSpec((1, 1, D, F), wmap),
            pl.BlockSpec((1, 1, D, F), wmap),
            pl.BlockSpec((1, 1, F, D), wmap),
        ],
        out_specs=pl.BlockSpec((tm * ROW_SUB, ROW_LANES), lambda i, te, tv, tb: (i, 0)),
        scratch_shapes=[pltpu.VMEM((D, F), BF16), pltpu.VMEM((D, F), BF16), pltpu.VMEM((F, D), BF16)],
    )
    return pl.pallas_call(
        functools.partial(_experts_kernel, tm=tm),
        grid_spec=grid_spec,
        out_shape=jax.ShapeDtypeStruct(xs.shape, ROW_DTYPE),
        compiler_params=_cparams(1),
        name="experts",
    )(tile_expert, tile_valid, tile_block, xs, wg, wu, wd)


def _ffn_out_kernel(pos_ref, next_pos_ref, x_ref, gate_ref, ys_hbm, sg_ref, su_ref, sd_ref, g_ref, b_ref, o_ref,
                    gbuf, sems, *, tc, alpha):
    i = pl.program_id(0)
    n_steps = pl.num_programs(0)
    slab = tc * ROW_SUB

    def start_gather(table_ref, slot):
        for k in range(TOP_K):
            def gather_rows(j, carry, k=k):
                for u in range(DMA_UNROLL):
                    t = j * DMA_UNROLL + u
                    pltpu.make_async_copy(
                        ys_hbm.at[pl.ds(pl.multiple_of(table_ref[k, t] * ROW_SUB, ROW_SUB), ROW_SUB)],
                        gbuf.at[slot, k, pl.ds(pl.multiple_of(t * ROW_SUB, ROW_SUB), ROW_SUB)],
                        sems.at[slot]).start(priority=u % 2)
                return carry
            lax.fori_loop(0, tc // DMA_UNROLL, gather_rows, 0)

    slot = i % 2

    @pl.when(i == 0)
    def _first_tile():
        start_gather(pos_ref, 0)

    @pl.when(i + 1 < n_steps)
    def _next_tile():
        start_gather(next_pos_ref, 1 - slot)

    x1 = x_ref[...]
    xb = x1.astype(BF16)
    hs = (jax.nn.silu(_dot(xb, sg_ref[...])) * _dot(xb, su_ref[...])).astype(BF16)
    ffn = _dot(hs, sd_ref[...])

    gate = gate_ref[...]
    for k in range(TOP_K):
        pltpu.make_async_copy(ys_hbm.at[pl.ds(0, slab)], gbuf.at[slot, k], sems.at[slot]).wait()
    for k in range(TOP_K):
        ffn = ffn + gate[:, k:k + 1] * _load_row_slabs(gbuf.at[slot, k], tc)
    o_ref[...] = _layer_norm(alpha * x1 + ffn, g_ref[...], b_ref[...])


def _ffn_out(pos8, x1, gate_tk, ys, sg, su, sd, g, b, *, tc, alpha):
    T, D = x1.shape
    n_steps = T // tc
    row = lambda i: (i, 0)
    return pl.pallas_call(
        functools.partial(_ffn_out_kernel, tc=tc, alpha=alpha),
        grid=(n_steps,),
        in_specs=[
            pl.BlockSpec((TOP_K, tc), lambda i: (0, i), memory_space=pltpu.SMEM),
            pl.BlockSpec((TOP_K, tc), lambda i: (0, jnp.minimum(i + 1, n_steps - 1)), memory_space=pltpu.SMEM),
            pl.BlockSpec((tc, D), row),
            pl.BlockSpec((tc, TOP_K), row),
            pl.BlockSpec(memory_space=pl.ANY),
            _const_spec(sg.shape), _const_spec(su.shape), _const_spec(sd.shape),
            _const_spec(g.shape), _const_spec(b.shape),
        ],
        out_specs=pl.BlockSpec((tc, D), row),
        out_shape=jax.ShapeDtypeStruct((T, D), F32),
        scratch_shapes=[pltpu.VMEM((2, TOP_K, tc * ROW_SUB, ROW_LANES), ROW_DTYPE), pltpu.SemaphoreType.DMA((2,))],
        compiler_params=_cparams(1),
        name="ffn_out",
    )(pos8, pos8, x1, gate_tk, ys, sg, su, sd, g, b)


def _tile_sizes(batch, seq):
    T = batch * seq
    return dict(
        tm=min(256, seq),
        tb=min(512, seq),
        tn=min(512, T),
        td=min(256, T),
        te=min(256, T),
        tc=min(128, T),
    )


def _rope_tables(seq):
    half = MLA_ROPE // 2
    inv = ROPE_THETA ** (-jnp.arange(half, dtype=F32) / half)
    ang = jnp.arange(seq, dtype=jnp.int32).astype(F32)[:, None] * inv[None, :]
    cos, sin = jnp.cos(ang), jnp.sin(ang)
    pad = jnp.zeros((seq, LANES - MLA_ROPE), F32)
    return jnp.concatenate([cos, cos, pad], axis=1), jnp.concatenate([-sin, sin, pad], axis=1)


def _pack_in_weight(w):
    D = w.shape[0]
    kr_end = 512 + 256 + MLA_ROPE
    dv_start = w.shape[1] - DIFF_HEADS * DIFF_V
    wa = jnp.concatenate([w[:, :kr_end], jnp.zeros((D, LANES - MLA_ROPE), w.dtype), w[:, kr_end:dv_start]], axis=1)
    return wa.astype(BF16), w[:, dv_start:].T.astype(BF16)


def _pack_uq(w):
    r = w.shape[0]
    w3 = w.reshape(r, MLA_HEADS, MLA_NOPE + MLA_ROPE)
    nope = w3[:, :, :MLA_NOPE].reshape(r, MLA_HEADS * MLA_NOPE)
    rope = jnp.pad(w3[:, :, MLA_NOPE:], ((0, 0), (0, 0), (0, LANES - MLA_ROPE))).reshape(r, MLA_HEADS * LANES)
    return jnp.concatenate([nope, rope], axis=1).astype(BF16)


def _pack_ukv(w):
    r = w.shape[0]
    w3 = w.reshape(r, MLA_HEADS, MLA_NOPE + MLA_V)
    return w3[:, :, :MLA_NOPE].reshape(r, -1).astype(BF16), w3[:, :, MLA_NOPE:].reshape(r, -1).T.astype(BF16)


def kernel(x, rel_bias, w_in, mla_q_norm, mla_w_uq, mla_kv_norm, mla_w_ukv, gmlp_ln_g, gmlp_ln_b, gmlp_w_s, gmlp_b_s, diff_lq1, diff_lk1, diff_lq2, diff_lk2, diff_subln, w_o, ln1_g, ln1_b, router_w, router_bias, exp_w_gate, exp_w_up, exp_w_down, shared_w_gate, shared_w_up, shared_w_down, ln2_g, ln2_b):
    B, S, D = x.shape
    T = B * S
    depth = w_in.shape[0]
    E = router_w.shape[-1]
    ts = _tile_sizes(B, S)
    alpha = (2 * depth) ** 0.25
    te = ts["te"]
    n_tiles = (T * TOP_K) // te + E

    cos, sin = _rope_tables(S)
    bias_tiles = _rel_bias_tiles(rel_bias, tb=ts["tb"])
    row2 = lambda a: a.reshape(1, -1)

    x2 = x.reshape(T, D)
    for l in range(depth):
        lambda_init = 0.8 - 0.6 * math.exp(-0.3 * l)
        wa, wdvt = _pack_in_weight(w_in[l])
        wuk, wuvt = _pack_ukv(mla_w_ukv[l])
        q, k, vt, og, dq, dk, dvt = _proj_in(
            x2, cos, sin, wa, wdvt, row2(mla_q_norm[l]), _pack_uq(mla_w_uq[l]),
            row2(mla_kv_norm[l]), wuk, wuvt, row2(gmlp_ln_g[l]), row2(gmlp_ln_b[l]),
            gmlp_w_s[l], gmlp_b_s[l].T, seq=S, tm=ts["tm"])
        om = _mla_attn(q, k, vt, batch=B, seq=S, tb=ts["tb"])
        od = _diff_attn(dq, dk, dvt, bias_tiles, row2(diff_lq1[l]), row2(diff_lk1[l]), row2(diff_lq2[l]),
                        row2(diff_lk2[l]), diff_subln[l].reshape(-1, 1), batch=B, seq=S, tb=ts["tb"],
                        lambda_init=lambda_init)
        rwh, rwl = _split_bf16(router_w[l].T)
        x1, x1s, logits_t = _proj_out(om, og, od, x2, w_o[l].astype(BF16), row2(ln1_g[l]), row2(ln1_b[l]),
                                      rwh, rwl, tm=ts["tm"], alpha=alpha)
        gates_t, sel_t = _route(logits_t, router_bias[l].reshape(E, 1), tn=ts["tn"])
        group_start, pad_start, pad_len, n_valid, tile_expert, tile_valid, tile_block = _group_layout(
            sel_t, tm=te, n_tiles=n_tiles)
        pos8, gate8 = _plan(sel_t, gates_t, group_start.astype(F32).reshape(E, 1), tn=ts["tn"])
        xs = _dispatch(pad_start, pad_len, n_valid, pos8, x1s, td=ts["td"], tm=te, n_tiles=n_tiles)
        ys = _experts(tile_expert, tile_valid, tile_block, xs, exp_w_gate, exp_w_up, exp_w_down, l, tm=te)
        x2 = _ffn_out(pos8, x1, gate8.T, ys, shared_w_gate[l].astype(BF16), shared_w_up[l].astype(BF16),
                      shared_w_down[l].astype(BF16), row2(ln2_g[l]), row2(ln2_b[l]), tc=ts["tc"], alpha=alpha)
    return x2.reshape(B, S, D)
```

```python
import functools
import math

import numpy as np
import jax
import jax.numpy as jnp
from jax import lax
from jax.experimental import pallas as pl
from jax.experimental.pallas import tpu as pltpu

F32 = jnp.float32
BF16 = jnp.bfloat16

MLA_HEADS = 8
MLA_NOPE = 128
MLA_ROPE = 64
MLA_V = 128
ROPE_THETA = 10000.0
GMLP_GROUPS = 4
GMLP_CHUNK = 128
DIFF_HEADS = 4
DIFF_QK = 64
DIFF_V = 128
REL_BUCKETS = 32
REL_MAX_EXACT = 16
REL_MAX_DIST = 128
N_GROUPS = 8
TOPK_GROUPS = 4
TOP_K = 8
ROUTED_SCALE = 2.5

LANES = 128
VMEM_LIMIT_BYTES = 56 * 1024 * 1024

NEG_INF = float("-inf")


def _cparams(n_grid_dims):
    return pltpu.CompilerParams(
        dimension_semantics=("arbitrary",) * n_grid_dims,
        vmem_limit_bytes=VMEM_LIMIT_BYTES,
    )


def _const_spec(shape):
    zeros = (0,) * len(shape)
    return pl.BlockSpec(shape, lambda *_: zeros, pipeline_mode=pl.Buffered(1))


def _dot(a, b):
    return jnp.dot(a, b, preferred_element_type=F32)


def _dot_nt(a, b):
    return lax.dot_general(a, b, (((1,), (1,)), ((), ())), preferred_element_type=F32)


def _rms(x, g, eps=1e-6):
    ms = jnp.mean(jnp.square(x), axis=-1, keepdims=True)
    return x * lax.rsqrt(ms + eps) * g


def _layer_norm(x, g, b, eps=1e-5):
    mu = jnp.mean(x, axis=-1, keepdims=True)
    xc = x - mu
    var = jnp.mean(jnp.square(xc), axis=-1, keepdims=True)
    return xc * lax.rsqrt(var + eps) * g + b


def _swap_rope_halves(x):
    n = x.shape[-1]
    lane = lax.broadcasted_iota(jnp.int32, x.shape, x.ndim - 1) % LANES
    half = MLA_ROPE // 2
    return jnp.where(lane < half, pltpu.roll(x, n - half, x.ndim - 1), pltpu.roll(x, half, x.ndim - 1))


_C_Q = (0, 512)
_C_KV = (512, 768)
_C_KR = (768, 896)
_C_GU = (896, 1408)
_C_GV = (1408, 1920)
_C_DQ = (1920, 2432)
_C_DK = (2432, 2944)

LOG2_E = math.log2(math.e)


def _proj_in_kernel(x_ref, cos_ref, sin_ref, wa_ref, wdvt_ref, qg_ref, wuq_ref, kvg_ref, wuk_ref, wuvt_ref,
                    lng_ref, lnb_ref, ws_ref, bst_ref,
                    q_out, k_out, vt_out, og_out, dq_out, dk_out, dvt_out, *, tm):
    xb = x_ref[...].astype(BF16)

    def proj(cols):
        return _dot(xb, wa_ref[:, cols[0]:cols[1]])

    cos = cos_ref[...]
    sin = sin_ref[...]
    mla_scale = (MLA_NOPE + MLA_ROPE) ** -0.5 * LOG2_E
    nope_w = MLA_HEADS * MLA_NOPE

    q = _dot(_rms(proj(_C_Q), qg_ref[...]).astype(BF16), wuq_ref[...])
    q_rope = q[:, nope_w:]
    cos8 = jnp.concatenate([cos] * MLA_HEADS, axis=-1)
    sin8 = jnp.concatenate([sin] * MLA_HEADS, axis=-1)
    q_rope = q_rope * cos8 + _swap_rope_halves(q_rope) * sin8
    for h in range(MLA_HEADS):
        q_out[:, 2 * LANES * h:2 * LANES * h + LANES] = (q[:, LANES * h:LANES * (h + 1)] * mla_scale).astype(BF16)
        q_out[:, 2 * LANES * h + LANES:2 * LANES * (h + 1)] = (
            q_rope[:, LANES * h:LANES * (h + 1)] * mla_scale).astype(BF16)

    kvn = _rms(proj(_C_KV), kvg_ref[...]).astype(BF16)
    k_nope = _dot(kvn, wuk_ref[...])
    k_rope = proj(_C_KR)
    k_rope = (k_rope * cos + _swap_rope_halves(k_rope) * sin).astype(BF16)
    for h in range(MLA_HEADS):
        k_out[:, 2 * LANES * h:2 * LANES * h + LANES] = k_nope[:, LANES * h:LANES * (h + 1)].astype(BF16)
        k_out[:, 2 * LANES * h + LANES:2 * LANES * (h + 1)] = k_rope
    vt_out[...] = _dot_nt(wuvt_ref[...], kvn).astype(BF16)

    u = jax.nn.gelu(proj(_C_GU))
    v = _layer_norm(jax.nn.gelu(proj(_C_GV)), lng_ref[...], lnb_ref[...])
    row = lax.broadcasted_iota(jnp.int32, (GMLP_CHUNK, GMLP_CHUNK), 0)
    col = lax.broadcasted_iota(jnp.int32, (GMLP_CHUNK, GMLP_CHUNK), 1)
    causal = col <= row
    for g in range(GMLP_GROUPS):
        ws = jnp.where(causal, ws_ref[g], 0.0).astype(BF16)
        bias = bst_ref[:, g:g + 1]
        for c in range(tm // GMLP_CHUNK):
            rows = slice(c * GMLP_CHUNK, (c + 1) * GMLP_CHUNK)
            cols = slice(g * LANES, (g + 1) * LANES)
            mixed = _dot(ws, v[rows, cols].astype(BF16)) + bias
            og_out[rows, cols] = (u[rows, cols] * mixed).astype(BF16)

    dq_out[...] = (proj(_C_DQ) * (DIFF_QK ** -0.5 * LOG2_E)).astype(BF16)
    dk_out[...] = proj(_C_DK).astype(BF16)
    dvt_out[...] = _dot_nt(wdvt_ref[...], xb).astype(BF16)


def _proj_in(x2, cos, sin, wa, wdvt, qg, wuq, kvg, wuk, wuvt, lng, lnb, ws, bst, *, seq, tm):
    T, D = x2.shape
    n_pos_blocks = seq // tm
    row = lambda i: (i, 0)
    col = lambda i: (0, i)
    pos = lambda i: (i % n_pos_blocks, 0)
    outs = ((2 * LANES * MLA_HEADS, False), (2 * LANES * MLA_HEADS, False), (MLA_V * MLA_HEADS, True),
            (GMLP_GROUPS * LANES, False), (DIFF_HEADS * LANES, False), (DIFF_HEADS * LANES, False),
            (DIFF_HEADS * DIFF_V, True))
    consts = (wa, wdvt, qg, wuq, kvg, wuk, wuvt, lng, lnb, ws, bst)
    return pl.pallas_call(
        functools.partial(_proj_in_kernel, tm=tm),
        grid=(T // tm,),
        in_specs=[
            pl.BlockSpec((tm, D), row),
            pl.BlockSpec((tm, LANES), pos),
            pl.BlockSpec((tm, LANES), pos),
        ] + [_const_spec(a.shape) for a in consts],
        out_specs=[pl.BlockSpec((w, tm), col) if t else pl.BlockSpec((tm, w), row) for w, t in outs],
        out_shape=[jax.ShapeDtypeStruct((w, T) if t else (T, w), BF16) for w, t in outs],
        compiler_params=_cparams(1),
        name="proj_in",
    )(x2, cos, sin, *consts)


def _causal_pairs(n_blocks):
    qi, ki = [], []
    for q in range(n_blocks):
        for k in range(q + 1):
            qi.append(q)
            ki.append(k)
    return np.asarray(qi, np.int32), np.asarray(ki, np.int32)


def _online_softmax_step(st, vt, m_ref, l_ref, acc_ref):
    m_prev = m_ref[...]
    m_new = jnp.maximum(m_prev, jnp.max(st, axis=0, keepdims=True))
    alpha = jnp.exp2(m_prev - m_new)
    p = jnp.exp2(st - m_new)
    l_ref[...] = alpha * l_ref[...] + jnp.sum(p, axis=0, keepdims=True)
    acc_ref[...] = alpha * acc_ref[...] + _dot(vt, p.astype(BF16))
    m_ref[...] = m_new


def _causal_keep_t(tb):
    key = lax.broadcasted_iota(jnp.int32, (tb, tb), 0)
    query = lax.broadcasted_iota(jnp.int32, (tb, tb), 1)
    return key <= query


def _mla_attn_kernel(qi_ref, ki_ref, q_ref, k_ref, vt_ref, o_ref, m_scr, l_scr, acc_scr, *, tb):
    p = pl.program_id(1)
    qi = qi_ref[p]
    ki = ki_ref[p]

    @pl.when(ki == 0)
    def _init():
        m_scr[...] = jnp.full(m_scr.shape, NEG_INF, F32)
        l_scr[...] = jnp.zeros(l_scr.shape, F32)
        acc_scr[...] = jnp.zeros(acc_scr.shape, F32)

    def step(on_diagonal):
        if on_diagonal:
            keep = _causal_keep_t(tb)
        for h in range(MLA_HEADS):
            qk = slice(2 * LANES * h, 2 * LANES * (h + 1))
            st = _dot_nt(k_ref[:, qk], q_ref[:, qk])
            if on_diagonal:
                st = jnp.where(keep, st, NEG_INF)
            _online_softmax_step(st, vt_ref[MLA_V * h:MLA_V * (h + 1), :], m_scr.at[h], l_scr.at[h], acc_scr.at[h])

    @pl.when(ki != qi)
    def _below():
        step(False)

    @pl.when(ki == qi)
    def _diag():
        step(True)
        for h in range(MLA_HEADS):
            o_ref[:, MLA_V * h:MLA_V * (h + 1)] = (acc_scr[h] / l_scr[h]).T.astype(BF16)


def _mla_attn(q, k, vt, *, batch, seq, tb):
    T = q.shape[0]
    W = vt.shape[0]
    nb = seq // tb
    qi_tab, ki_tab = _causal_pairs(nb)
    qmap = lambda b, p, qi, ki: (b * nb + qi[p], 0)
    kmap = lambda b, p, qi, ki: (b * nb + ki[p], 0)
    vmap = lambda b, p, qi, ki: (0, b * nb + ki[p])
    grid_spec = pltpu.PrefetchScalarGridSpec(
        num_scalar_prefetch=2,
        grid=(batch, len(qi_tab)),
        in_specs=[
            pl.BlockSpec((tb, q.shape[1]), qmap),
            pl.BlockSpec((tb, k.shape[1]), kmap),
            pl.BlockSpec((W, tb), vmap),
        ],
        out_specs=pl.BlockSpec((tb, W), qmap),
        scratch_shapes=[
            pltpu.VMEM((MLA_HEADS, 1, tb), F32),
            pltpu.VMEM((MLA_HEADS, 1, tb), F32),
            pltpu.VMEM((MLA_HEADS, MLA_V, tb), F32),
        ],
    )
    return pl.pallas_call(
        functools.partial(_mla_attn_kernel, tb=tb),
        grid_spec=grid_spec,
        out_shape=jax.ShapeDtypeStruct((T, W), BF16),
        compiler_params=_cparams(2),
        name="mla_attn",
    )(jnp.asarray(qi_tab), jnp.asarray(ki_tab), q, k, vt)


def _t5_causal_bucket(n):
    is_small = n < REL_MAX_EXACT
    nf = jnp.maximum(n, 1).astype(F32)
    large = REL_MAX_EXACT + (jnp.log(nf / REL_MAX_EXACT) / math.log(REL_MAX_DIST / REL_MAX_EXACT)
                             * (REL_BUCKETS - REL_MAX_EXACT)).astype(jnp.int32)
    large = jnp.minimum(large, REL_BUCKETS - 1)
    return jnp.where(is_small, n, large)


def _rel_bias_kernel(tab_ref, o_ref, *, tb):
    slab = pl.program_id(0)
    key = lax.broadcasted_iota(jnp.int32, (tb, tb), 0)
    query = lax.broadcasted_iota(jnp.int32, (tb, tb), 1)
    bucket = _t5_causal_bucket(jnp.maximum(slab * tb + query - key, 0))
    for h in range(DIFF_HEADS):
        far = tab_ref[REL_BUCKETS - 1, h]
        acc = jnp.zeros((tb, tb), F32)
        for b in range(REL_BUCKETS - 1):
            acc = jnp.where(bucket == b, (tab_ref[b, h] - far) * LOG2_E, acc)
        o_ref[h, 0] = acc


def _rel_bias_tiles(rel_bias, *, tb):
    return pl.pallas_call(
        functools.partial(_rel_bias_kernel, tb=tb),
        grid=(2,),
        in_specs=[pl.BlockSpec(memory_space=pltpu.SMEM)],
        out_specs=pl.BlockSpec((DIFF_HEADS, 1, tb, tb), lambda s: (0, s, 0, 0)),
        out_shape=jax.ShapeDtypeStruct((DIFF_HEADS, 2, tb, tb), F32),
        compiler_params=_cparams(1),
        name="rel_bias",
    )(rel_bias)


def _diff_attn_kernel(qi_ref, ki_ref, q_ref, k_ref, vt_ref, bias_ref, lq1_ref, lk1_ref, lq2_ref, lk2_ref, g_ref,
                      o_ref, m_scr, l_scr, acc_scr, *, tb, lambda_init):
    p = pl.program_id(1)
    qi = qi_ref[p]
    ki = ki_ref[p]

    @pl.when(ki == 0)
    def _init():
        m_scr[...] = jnp.full(m_scr.shape, NEG_INF, F32)
        l_scr[...] = jnp.zeros(l_scr.shape, F32)
        acc_scr[...] = jnp.zeros(acc_scr.shape, F32)

    def step(slab):
        if slab == 0:
            keep = _causal_keep_t(tb)
        lane = lax.broadcasted_iota(jnp.int32, (tb, LANES), 1)
        for h in range(DIFF_HEADS):
            hs = slice(LANES * h, LANES * (h + 1))
            qh = q_ref[:, hs]
            kh = k_ref[:, hs]
            vth = vt_ref[DIFF_V * h:DIFF_V * (h + 1), :]
            for m in range(2):
                in_map = (lane < DIFF_QK) if m == 0 else (lane >= DIFF_QK)
                st = _dot_nt(kh, jnp.where(in_map, qh, jnp.zeros_like(qh)))
                if slab is not None:
                    st = st + bias_ref[h, slab]
                if slab == 0:
                    st = jnp.where(keep, st, NEG_INF)
                i = 2 * h + m
                _online_softmax_step(st, vth, m_scr.at[i], l_scr.at[i], acc_scr.at[i])

    @pl.when(ki < qi - 1)
    def _far():
        step(None)

    @pl.when(ki == qi - 1)
    def _near():
        step(1)

    @pl.when(ki == qi)
    def _diag():
        step(0)
        lam = (jnp.exp(jnp.sum(lq1_ref[...] * lk1_ref[...], axis=-1, keepdims=True))
               - jnp.exp(jnp.sum(lq2_ref[...] * lk2_ref[...], axis=-1, keepdims=True)) + lambda_init)
        for h in range(DIFF_HEADS):
            ot = acc_scr[2 * h] / l_scr[2 * h] - lam * (acc_scr[2 * h + 1] / l_scr[2 * h + 1])
            ms = jnp.mean(jnp.square(ot), axis=0, keepdims=True)
            ot = ot * lax.rsqrt(ms + 1e-6) * g_ref[...] * (1.0 - lambda_init)
            o_ref[:, DIFF_V * h:DIFF_V * (h + 1)] = ot.T.astype(BF16)


def _diff_attn(q, k, vt, bias_tiles, lq1, lk1, lq2, lk2, subln_col, *, batch, seq, tb, lambda_init):
    T, W = q.shape
    nb = seq // tb
    qi_tab, ki_tab = _causal_pairs(nb)
    qmap = lambda b, p, qi, ki: (b * nb + qi[p], 0)
    kmap = lambda b, p, qi, ki: (b * nb + ki[p], 0)
    vmap = lambda b, p, qi, ki: (0, b * nb + ki[p])
    small = lambda a: _const_spec(a.shape)
    grid_spec = pltpu.PrefetchScalarGridSpec(
        num_scalar_prefetch=2,
        grid=(batch, len(qi_tab)),
        in_specs=[
            pl.BlockSpec((tb, W), qmap),
            pl.BlockSpec((tb, W), kmap),
            pl.BlockSpec((vt.shape[0], tb), vmap),
            small(bias_tiles), small(lq1), small(lk1), small(lq2), small(lk2), small(subln_col),
        ],
        out_specs=pl.BlockSpec((tb, vt.shape[0]), qmap),
        scratch_shapes=[
            pltpu.VMEM((2 * DIFF_HEADS, 1, tb), F32),
            pltpu.VMEM((2 * DIFF_HEADS, 1, tb), F32),
            pltpu.VMEM((2 * DIFF_HEADS, DIFF_V, tb), F32),
        ],
    )
    return pl.pallas_call(
        functools.partial(_diff_attn_kernel, tb=tb, lambda_init=lambda_init),
        grid_spec=grid_spec,
        out_shape=jax.ShapeDtypeStruct((T, vt.shape[0]), BF16),
        compiler_params=_cparams(2),
        name="diff_attn",
    )(jnp.asarray(qi_tab), jnp.asarray(ki_tab), q, k, vt, bias_tiles, lq1, lk1, lq2, lk2, subln_col)


ROW_SUB = 8
ROW_LANES = 128
ROW_DTYPE = jnp.int32
_HIGH_HALF = -65536


def _split_bf16(x):
    hi = x.astype(BF16)
    lo = (x - hi.astype(F32)).astype(BF16)
    return hi, lo


def _f32_bits(x):
    return lax.bitcast_convert_type(x, jnp.int32)


def _store_row_slabs(dst_ref, val, tm):
    half = ROW_SUB * ROW_LANES
    for c in range(ROW_SUB):
        lo = _f32_bits(val[:, ROW_LANES * c:ROW_LANES * (c + 1)].astype(BF16).astype(F32))
        hi = _f32_bits(val[:, half + ROW_LANES * c:half + ROW_LANES * (c + 1)].astype(BF16).astype(F32))
        word = lax.shift_right_logical(lo, jnp.full_like(lo, 16)) | (hi & _HIGH_HALF)
        dst_ref[pl.ds(c, tm, stride=ROW_SUB), :] = word


def _load_row_slabs(src_ref, tm):
    lows, highs = [], []
    for c in range(ROW_SUB):
        word = src_ref[pl.ds(c, tm, stride=ROW_SUB), :]
        lows.append(lax.bitcast_convert_type(word << 16, F32))
        highs.append(lax.bitcast_convert_type(word & _HIGH_HALF, F32))
    return jnp.concatenate(lows + highs, axis=-1)


def _proj_out_kernel(om_ref, og_ref, od_ref, x_ref, wo_ref, g_ref, b_ref, rwh_ref, rwl_ref,
                     x1_out, x1s_out, logit_out, *, tm, alpha):
    n_m = om_ref.shape[1]
    n_g = og_ref.shape[1]
    mix = (_dot(om_ref[...], wo_ref[0:n_m, :])
           + _dot(og_ref[...], wo_ref[n_m:n_m + n_g, :])
           + _dot(od_ref[...], wo_ref[n_m + n_g:, :]))
    x1 = _layer_norm(alpha * x_ref[...] + mix, g_ref[...], b_ref[...])
    x1_out[...] = x1
    _store_row_slabs(x1s_out, x1, tm)
    xh, xl = _split_bf16(x1)
    logit_out[...] = _dot_nt(rwh_ref[...], xh) + _dot_nt(rwh_ref[...], xl) + _dot_nt(rwl_ref[...], xh)


def _proj_out(om, og, od, x2, wo, g, b, rwh, rwl, *, tm, alpha):
    T, D = x2.shape
    E = rwh.shape[0]
    row = lambda i: (i, 0)
    return pl.pallas_call(
        functools.partial(_proj_out_kernel, tm=tm, alpha=alpha),
        grid=(T // tm,),
        in_specs=[
            pl.BlockSpec((tm, om.shape[1]), row), pl.BlockSpec((tm, og.shape[1]), row),
            pl.BlockSpec((tm, od.shape[1]), row), pl.BlockSpec((tm, D), row),
            _const_spec(wo.shape), _const_spec(g.shape), _const_spec(b.shape),
            _const_spec(rwh.shape), _const_spec(rwl.shape),
        ],
        out_specs=[
            pl.BlockSpec((tm, D), row),
            pl.BlockSpec((tm * ROW_SUB, ROW_LANES), row),
            pl.BlockSpec((E, tm), lambda i: (0, i)),
        ],
        out_shape=[
            jax.ShapeDtypeStruct((T, D), F32),
            jax.ShapeDtypeStruct((T * ROW_SUB, ROW_LANES), ROW_DTYPE),
            jax.ShapeDtypeStruct((E, T), F32),
        ],
        compiler_params=_cparams(1),
        name="proj_out",
    )(om, og, od, x2, wo, g, b, rwh, rwl)


def _first_argmax(v, axis, size):
    mx = jnp.max(v, axis=axis, keepdims=True)
    idx = lax.broadcasted_iota(jnp.int32, v.shape, axis)
    first = jnp.min(jnp.where(v == mx, idx, size), axis=axis, keepdims=True)
    return mx, first


def _route_kernel(logit_ref, bias_ref, gate_out, sel_out):
    E, tn = logit_ref.shape
    per_group = E // N_GROUPS
    scores = jax.nn.sigmoid(logit_ref[...])
    biased = scores + bias_ref[...]
    grouped = biased.reshape(N_GROUPS, per_group, tn)
    member = lax.broadcasted_iota(jnp.int32, grouped.shape, 1)
    top1, first = _first_argmax(grouped, 1, per_group)
    top2 = jnp.max(jnp.where(member == first, NEG_INF, grouped), axis=1, keepdims=True)
    group_score = (top1 + top2).reshape(N_GROUPS, tn)

    group_id = lax.broadcasted_iota(jnp.int32, group_score.shape, 0)
    group_sel = jnp.zeros(group_score.shape, F32)
    remaining = group_score
    for _ in range(TOPK_GROUPS):
        _, first = _first_argmax(remaining, 0, N_GROUPS)
        hit = group_id == first
        group_sel = jnp.where(hit, 1.0, group_sel)
        remaining = jnp.where(hit, NEG_INF, remaining)

    masked = jnp.where(group_sel.reshape(N_GROUPS, 1, tn) > 0.5, grouped, NEG_INF).reshape(E, tn)
    expert_id = lax.broadcasted_iota(jnp.int32, masked.shape, 0)
    sel = jnp.zeros(masked.shape, F32)
    for _ in range(TOP_K):
        _, first = _first_argmax(masked, 0, E)
        hit = expert_id == first
        sel = jnp.where(hit, 1.0, sel)
        masked = jnp.where(hit, NEG_INF, masked)

    w = scores * sel
    gate_out[...] = w / jnp.sum(w, axis=0, keepdims=True) * ROUTED_SCALE
    sel_out[...] = sel


def _route(logits_t, bias_col, *, tn):
    E, T = logits_t.shape
    col = lambda i: (0, i)
    return pl.pallas_call(
        _route_kernel,
        grid=(T // tn,),
        in_specs=[pl.BlockSpec((E, tn), col), _const_spec(bias_col.shape)],
        out_specs=[pl.BlockSpec((E, tn), col), pl.BlockSpec((E, tn), col)],
        out_shape=[jax.ShapeDtypeStruct((E, T), F32), jax.ShapeDtypeStruct((E, T), F32)],
        compiler_params=_cparams(1),
        name="route",
    )(logits_t, bias_col)


def _plan_kernel(sel_ref, gate_ref, start_ref, pos_out, gate_out, carry):
    E, tn = sel_ref.shape

    @pl.when(pl.program_id(0) == 0)
    def _init():
        carry[...] = jnp.zeros(carry.shape, F32)

    sel = sel_ref[...]
    selb = sel.astype(BF16)
    earlier_token = (lax.broadcasted_iota(jnp.int32, (tn, tn), 0) < lax.broadcasted_iota(jnp.int32, (tn, tn), 1))
    rank = _dot(selb, jnp.where(earlier_token, 1.0, 0.0).astype(BF16)) + carry[...]
    smaller_expert = (lax.broadcasted_iota(jnp.int32, (E, E), 1) < lax.broadcasted_iota(jnp.int32, (E, E), 0))
    slot = _dot(jnp.where(smaller_expert, 1.0, 0.0).astype(BF16), selb)
    pos = start_ref[...] + rank
    gates = gate_ref[...]
    for k in range(TOP_K):
        hit = jnp.logical_and(sel > 0.5, slot == float(k))
        pos_out[k:k + 1, :] = jnp.sum(jnp.where(hit, pos, 0.0), axis=0, keepdims=True).astype(jnp.int32)
        gate_out[k:k + 1, :] = jnp.sum(jnp.where(hit, gates, 0.0), axis=0, keepdims=True)
    carry[...] = carry[...] + jnp.sum(sel, axis=1, keepdims=True)


def _plan(sel_t, gates_t, group_start_col, *, tn):
    E, T = sel_t.shape
    col = lambda i: (0, i)
    return pl.pallas_call(
        _plan_kernel,
        grid=(T // tn,),
        in_specs=[pl.BlockSpec((E, tn), col), pl.BlockSpec((E, tn), col), _const_spec(group_start_col.shape)],
        out_specs=[pl.BlockSpec((TOP_K, tn), col), pl.BlockSpec((TOP_K, tn), col)],
        out_shape=[jax.ShapeDtypeStruct((TOP_K, T), jnp.int32), jax.ShapeDtypeStruct((TOP_K, T), F32)],
        scratch_shapes=[pltpu.VMEM((E, 1), F32)],
        compiler_params=_cparams(1),
        name="plan",
    )(sel_t, gates_t, group_start_col)


def _group_layout(sel_t, *, tm, n_tiles):
    E = sel_t.shape[0]
    counts = jnp.sum(sel_t, axis=1).astype(jnp.int32)
    padded = (counts + tm - 1) // tm * tm
    group_end = jnp.cumsum(padded)
    group_start = group_end - padded
    tile_start = jnp.arange(n_tiles, dtype=jnp.int32) * tm
    tile_expert = jnp.sum((tile_start[:, None] >= group_end[None, :]).astype(jnp.int32), axis=1)
    tile_valid = tile_expert < E
    n_valid = jnp.sum(tile_valid.astype(jnp.int32))
    last_expert = jnp.max(jnp.where(counts > 0, jnp.arange(E, dtype=jnp.int32), 0))
    tile_expert = jnp.where(tile_valid, tile_expert, last_expert).astype(jnp.int32)
    tile_block = jnp.minimum(jnp.arange(n_tiles, dtype=jnp.int32), jnp.maximum(n_valid - 1, 0))
    pad_start = (group_start + counts).astype(jnp.int32)
    pad_len = (padded - counts).astype(jnp.int32)
    return (group_start, pad_start, pad_len, n_valid.reshape(1).astype(jnp.int32),
            tile_expert, tile_valid.astype(jnp.int32), tile_block)


def _dispatch_kernel(pad_ref, padlen_ref, nvalid_ref, pos_ref, xs_ref, x_ref, sg_ref, su_ref, sd_ref,
                     xs_hbm, shared_out, zbuf, sems, *, td, tm, n_experts, n_tiles):
    i = pl.program_id(0)

    def zero_fill(wait):
        def piece(row, n_rows):
            copy = pltpu.make_async_copy(
                zbuf.at[pl.ds(0, n_rows * ROW_SUB)],
                xs_hbm.at[pl.ds(pl.multiple_of(row * ROW_SUB, ROW_SUB), n_rows * ROW_SUB)], sems.at[0])
            if wait:
                copy.wait()
            else:
                copy.start()

        def expert_padding(e, carry):
            row = pad_ref[e]
            n = padlen_ref[e]
            size = tm // 2
            while size >= 1:
                pl.when((n & size) != 0)(functools.partial(piece, row, size))
                row = row + (n & size)
                size //= 2
            return carry
        lax.fori_loop(0, n_experts, expert_padding, 0)

        def unused_tile(j, carry):
            piece(j * tm, tm)
            return carry
        lax.fori_loop(nvalid_ref[0], n_tiles, unused_tile, 0)

    @pl.when(i == 0)
    def _start_zero_fill():
        zbuf[...] = jnp.zeros(zbuf.shape, ROW_DTYPE)
        zero_fill(wait=False)

    for t in range(td):
        for k in range(TOP_K):
            pltpu.make_async_copy(
                xs_ref.at[pl.ds(t * ROW_SUB, ROW_SUB)],
                xs_hbm.at[pl.ds(pl.multiple_of(pos_ref[k, t] * ROW_SUB, ROW_SUB), ROW_SUB)],
                sems.at[1]).start(priority=k % 2)

    xb = x_ref[...].astype(BF16)
    hs = (jax.nn.silu(_dot(xb, sg_ref[...])) * _dot(xb, su_ref[...])).astype(BF16)
    shared_out[...] = _dot(hs, sd_ref[...])

    for k in range(TOP_K):
        pltpu.make_async_copy(xs_ref, xs_hbm.at[pl.ds(0, td * ROW_SUB)], sems.at[1]).wait()

    @pl.when(i == 0)
    def _finish_zero_fill():
        zero_fill(wait=True)


def _dispatch(pad_start, pad_len, n_valid, pos8, x1s, x1, sg, su, sd, *, td, tm, n_tiles):
    T, D = x1.shape
    E = pad_start.shape[0]
    row = lambda i, *_: (i, 0)
    grid_spec = pltpu.PrefetchScalarGridSpec(
        num_scalar_prefetch=3,
        grid=(T // td,),
        in_specs=[
            pl.BlockSpec((TOP_K, td), lambda i, *_: (0, i), memory_space=pltpu.SMEM),
            pl.BlockSpec((td * ROW_SUB, ROW_LANES), row),
            pl.BlockSpec((td, D), row),
            _const_spec(sg.shape), _const_spec(su.shape), _const_spec(sd.shape),
        ],
        out_specs=[pl.BlockSpec(memory_space=pl.ANY), pl.BlockSpec((td, D), row)],
        scratch_shapes=[pltpu.VMEM((tm * ROW_SUB, ROW_LANES), ROW_DTYPE), pltpu.SemaphoreType.DMA((2,))],
    )
    return pl.pallas_call(
        functools.partial(_dispatch_kernel, td=td, tm=tm, n_experts=E, n_tiles=n_tiles),
        grid_spec=grid_spec,
        out_shape=[jax.ShapeDtypeStruct((n_tiles * tm * ROW_SUB, ROW_LANES), ROW_DTYPE),
                   jax.ShapeDtypeStruct((T, D), F32)],
        compiler_params=_cparams(1),
        name="dispatch",
    )(pad_start, pad_len, n_valid, pos8, x1s, x1, sg, su, sd)


def _experts_kernel(te_ref, tv_ref, tb_ref, xs_ref, wg_ref, wu_ref, wd_ref, ys_ref, wgb, wub, wdb, *, tm):
    i = pl.program_id(0)

    @pl.when(tv_ref[i] > 0)
    def _tile():
        @pl.when(jnp.logical_or(i == 0, te_ref[i] != te_ref[jnp.maximum(i - 1, 0)]))
        def _cast_weights():
            wgb[...] = wg_ref[0, 0].astype(BF16)
            wub[...] = wu_ref[0, 0].astype(BF16)
            wdb[...] = wd_ref[0, 0].astype(BF16)

        xb = _load_row_slabs(xs_ref, tm).astype(BF16)
        h = (jax.nn.silu(_dot(xb, wgb[...])) * _dot(xb, wub[...])).astype(BF16)
        _store_row_slabs(ys_ref, _dot(h, wdb[...]), tm)

    @pl.when(tv_ref[i] == 0)
    def _unused_tile():
        ys_ref[...] = jnp.zeros(ys_ref.shape, ROW_DTYPE)


def _experts(tile_expert, tile_valid, tile_block, xs, wg, wu, wd, layer, *, tm):
    n_tiles = tile_expert.shape[0]
    D, F = wg.shape[2], wg.shape[3]
    wmap = lambda i, te, tv, tb: (layer, te[i], 0, 0)
    grid_spec = pltpu.PrefetchScalarGridSpec(
        num_scalar_prefetch=3,
        grid=(n_tiles,),
        in_specs=[
            pl.BlockSpec((tm * ROW_SUB, ROW_LANES), lambda i, te, tv, tb: (tb[i], 0)),
            pl.BlockSpec((1, 1, D, F), wmap),
            pl.BlockSpec((1, 1, D, F), wmap),
            pl.BlockSpec((1, 1, F, D), wmap),
        ],
        out_specs=pl.BlockSpec((tm * ROW_SUB, ROW_LANES), lambda i, te, tv, tb: (i, 0)),
        scratch_shapes=[pltpu.VMEM((D, F), BF16), pltpu.VMEM((D, F), BF16), pltpu.VMEM((F, D), BF16)],
    )
    return pl.pallas_call(
        functools.partial(_experts_kernel, tm=tm),
        grid_spec=grid_spec,
        out_shape=jax.ShapeDtypeStruct(xs.shape, ROW_DTYPE),
        compiler_params=_cparams(1),
        name="experts",
    )(tile_expert, tile_valid, tile_block, xs, wg, wu, wd)


def _ffn_out_kernel(pos_ref, next_pos_ref, x_ref, shared_ref, gate_ref, ys_hbm, g_ref, b_ref, o_ref,
                    gbuf, sems, *, tc, alpha):
    i = pl.program_id(0)
    n_steps = pl.num_programs(0)
    slab = tc * ROW_SUB

    def start_gather(table_ref, slot):
        for t in range(tc):
            for k in range(TOP_K):
                pltpu.make_async_copy(
                    ys_hbm.at[pl.ds(pl.multiple_of(table_ref[k, t] * ROW_SUB, ROW_SUB), ROW_SUB)],
                    gbuf.at[slot, k, pl.ds(t * ROW_SUB, ROW_SUB)],
                    sems.at[slot]).start(priority=k % 2)

    slot = i % 2

    @pl.when(i == 0)
    def _first_tile():
        start_gather(pos_ref, 0)

    @pl.when(i + 1 < n_steps)
    def _next_tile():
        start_gather(next_pos_ref, 1 - slot)

    x1 = x_ref[...]
    ffn = shared_ref[...]
    gate = gate_ref[...]
    for k in range(TOP_K):
        pltpu.make_async_copy(ys_hbm.at[pl.ds(0, slab)], gbuf.at[slot, k], sems.at[slot]).wait()
    for k in range(TOP_K):
        ffn = ffn + gate[:, k:k + 1] * _load_row_slabs(gbuf.at[slot, k], tc)
    o_ref[...] = _layer_norm(alpha * x1 + ffn, g_ref[...], b_ref[...])


def _ffn_out(pos8, x1, shared, gate_tk, ys, g, b, *, tc, alpha):
    T, D = x1.shape
    n_steps = T // tc
    row = lambda i: (i, 0)
    return pl.pallas_call(
        functools.partial(_ffn_out_kernel, tc=tc, alpha=alpha),
        grid=(n_steps,),
        in_specs=[
            pl.BlockSpec((TOP_K, tc), lambda i: (0, i), memory_space=pltpu.SMEM),
            pl.BlockSpec((TOP_K, tc), lambda i: (0, jnp.minimum(i + 1, n_steps - 1)), memory_space=pltpu.SMEM),
            pl.BlockSpec((tc, D), row),
            pl.BlockSpec((tc, D), row),
            pl.BlockSpec((tc, TOP_K), row),
            pl.BlockSpec(memory_space=pl.ANY),
            _const_spec(g.shape), _const_spec(b.shape),
        ],
        out_specs=pl.BlockSpec((tc, D), row),
        out_shape=jax.ShapeDtypeStruct((T, D), F32),
        scratch_shapes=[pltpu.VMEM((2, TOP_K, tc * ROW_SUB, ROW_LANES), ROW_DTYPE), pltpu.SemaphoreType.DMA((2,))],
        compiler_params=_cparams(1),
        name="ffn_out",
    )(pos8, pos8, x1, shared, gate_tk, ys, g, b)


def _tile_sizes(batch, seq):
    T = batch * seq
    return dict(
        tm=min(256, seq),
        tb=min(512, seq),
        tn=min(512, T),
        td=min(256, T),
        te=min(256, T),
        tc=min(128, T),
    )


def _rope_tables(seq):
    half = MLA_ROPE // 2
    inv = ROPE_THETA ** (-jnp.arange(half, dtype=F32) / half)
    ang = jnp.arange(seq, dtype=jnp.int32).astype(F32)[:, None] * inv[None, :]
    cos, sin = jnp.cos(ang), jnp.sin(ang)
    pad = jnp.zeros((seq, LANES - MLA_ROPE), F32)
    return jnp.concatenate([cos, cos, pad], axis=1), jnp.concatenate([-sin, sin, pad], axis=1)


def _pack_in_weight(w):
    D = w.shape[0]
    kr_end = 512 + 256 + MLA_ROPE
    dv_start = w.shape[1] - DIFF_HEADS * DIFF_V
    wa = jnp.concatenate([w[:, :kr_end], jnp.zeros((D, LANES - MLA_ROPE), w.dtype), w[:, kr_end:dv_start]], axis=1)
    return wa.astype(BF16), w[:, dv_start:].T.astype(BF16)


def _pack_uq(w):
    r = w.shape[0]
    w3 = w.reshape(r, MLA_HEADS, MLA_NOPE + MLA_ROPE)
    nope = w3[:, :, :MLA_NOPE].reshape(r, MLA_HEADS * MLA_NOPE)
    rope = jnp.pad(w3[:, :, MLA_NOPE:], ((0, 0), (0, 0), (0, LANES - MLA_ROPE))).reshape(r, MLA_HEADS * LANES)
    return jnp.concatenate([nope, rope], axis=1).astype(BF16)


def _pack_ukv(w):
    r = w.shape[0]
    w3 = w.reshape(r, MLA_HEADS, MLA_NOPE + MLA_V)
    return w3[:, :, :MLA_NOPE].reshape(r, -1).astype(BF16), w3[:, :, MLA_NOPE:].reshape(r, -1).T.astype(BF16)


def kernel(x, rel_bias, w_in, mla_q_norm, mla_w_uq, mla_kv_norm, mla_w_ukv, gmlp_ln_g, gmlp_ln_b, gmlp_w_s, gmlp_b_s, diff_lq1, diff_lk1, diff_lq2, diff_lk2, diff_subln, w_o, ln1_g, ln1_b, router_w, router_bias, exp_w_gate, exp_w_up, exp_w_down, shared_w_gate, shared_w_up, shared_w_down, ln2_g, ln2_b):
    B, S, D = x.shape
    T = B * S
    depth = w_in.shape[0]
    E = router_w.shape[-1]
    ts = _tile_sizes(B, S)
    alpha = (2 * depth) ** 0.25
    te = ts["te"]
    n_tiles = (T * TOP_K) // te + E

    cos, sin = _rope_tables(S)
    bias_tiles = _rel_bias_tiles(rel_bias, tb=ts["tb"])
    row2 = lambda a: a.reshape(1, -1)

    x2 = x.reshape(T, D)
    for l in range(depth):
        lambda_init = 0.8 - 0.6 * math.exp(-0.3 * l)
        wa, wdvt = _pack_in_weight(w_in[l])
        wuk, wuvt = _pack_ukv(mla_w_ukv[l])
        q, k, vt, og, dq, dk, dvt = _proj_in(
            x2, cos, sin, wa, wdvt, row2(mla_q_norm[l]), _pack_uq(mla_w_uq[l]),
            row2(mla_kv_norm[l]), wuk, wuvt, row2(gmlp_ln_g[l]), row2(gmlp_ln_b[l]),
            gmlp_w_s[l], gmlp_b_s[l].T, seq=S, tm=ts["tm"])
        om = _mla_attn(q, k, vt, batch=B, seq=S, tb=ts["tb"])
        od = _diff_attn(dq, dk, dvt, bias_tiles, row2(diff_lq1[l]), row2(diff_lk1[l]), row2(diff_lq2[l]),
                        row2(diff_lk2[l]), diff_subln[l].reshape(-1, 1), batch=B, seq=S, tb=ts["tb"],
                        lambda_init=lambda_init)
        rwh, rwl = _split_bf16(router_w[l].T)
        x1, x1s, logits_t = _proj_out(om, og, od, x2, w_o[l].astype(BF16), row2(ln1_g[l]), row2(ln1_b[l]),
                                      rwh, rwl, tm=ts["tm"], alpha=alpha)
        gates_t, sel_t = _route(logits_t, router_bias[l].reshape(E, 1), tn=ts["tn"])
        group_start, pad_start, pad_len, n_valid, tile_expert, tile_valid, tile_block = _group_layout(
            sel_t, tm=te, n_tiles=n_tiles)
        pos8, gate8 = _plan(sel_t, gates_t, group_start.astype(F32).reshape(E, 1), tn=ts["tn"])
        xs, shared = _dispatch(pad_start, pad_len, n_valid, pos8, x1s, x1, shared_w_gate[l].astype(BF16),
                               shared_w_up[l].astype(BF16), shared_w_down[l].astype(BF16),
                               td=ts["td"], tm=te, n_tiles=n_tiles)
        ys = _experts(tile_expert, tile_valid, tile_block, xs, exp_w_gate, exp_w_up, exp_w_down, l, tm=te)
        x2 = _ffn_out(pos8, x1, shared, gate8.T, ys, row2(ln2_g[l]), row2(ln2_b[l]), tc=ts["tc"], alpha=alpha)
    return x2.reshape(B, S, D)
```

```python
import functools
import math

import numpy as np
import jax
import jax.numpy as jnp
from jax import lax
from jax.experimental import pallas as pl
from jax.experimental.pallas import tpu as pltpu

F32 = jnp.float32
BF16 = jnp.bfloat16

MLA_HEADS = 8
MLA_NOPE = 128
MLA_ROPE = 64
MLA_V = 128
ROPE_THETA = 10000.0
GMLP_GROUPS = 4
GMLP_CHUNK = 128
DIFF_HEADS = 4
DIFF_QK = 64
DIFF_V = 128
REL_BUCKETS = 32
REL_MAX_EXACT = 16
REL_MAX_DIST = 128
N_GROUPS = 8
TOPK_GROUPS = 4
TOP_K = 8
ROUTED_SCALE = 2.5

LANES = 128
VMEM_LIMIT_BYTES = 56 * 1024 * 1024

NEG_INF = float("-inf")


def _cparams(n_grid_dims):
    return pltpu.CompilerParams(
        dimension_semantics=("arbitrary",) * n_grid_dims,
        vmem_limit_bytes=VMEM_LIMIT_BYTES,
    )


def _const_spec(shape):
    zeros = (0,) * len(shape)
    return pl.BlockSpec(shape, lambda *_: zeros, pipeline_mode=pl.Buffered(1))


def _dot(a, b):
    return jnp.dot(a, b, preferred_element_type=F32)


def _dot_nt(a, b):
    return lax.dot_general(a, b, (((1,), (1,)), ((), ())), preferred_element_type=F32)


def _rms(x, g, eps=1e-6):
    ms = jnp.mean(jnp.square(x), axis=-1, keepdims=True)
    return x * lax.rsqrt(ms + eps) * g


def _layer_norm(x, g, b, eps=1e-5):
    mu = jnp.mean(x, axis=-1, keepdims=True)
    xc = x - mu
    var = jnp.mean(jnp.square(xc), axis=-1, keepdims=True)
    return xc * lax.rsqrt(var + eps) * g + b


def _swap_rope_halves(x):
    n = x.shape[-1]
    lane = lax.broadcasted_iota(jnp.int32, x.shape, x.ndim - 1) % LANES
    half = MLA_ROPE // 2
    return jnp.where(lane < half, pltpu.roll(x, n - half, x.ndim - 1), pltpu.roll(x, half, x.ndim - 1))


_C_Q = (0, 512)
_C_KV = (512, 768)
_C_KR = (768, 896)
_C_GU = (896, 1408)
_C_GV = (1408, 1920)
_C_DQ = (1920, 2432)
_C_DK = (2432, 2944)

LOG2_E = math.log2(math.e)


def _proj_in_kernel(x_ref, cos_ref, sin_ref, wa_ref, wdvt_ref, qg_ref, wuq_ref, kvg_ref, wuk_ref, wuvt_ref,
                    lng_ref, lnb_ref, ws_ref, bst_ref,
                    q_out, k_out, vt_out, og_out, dq_out, dk_out, dvt_out, *, tm):
    xb = x_ref[...].astype(BF16)

    def proj(cols):
        return _dot(xb, wa_ref[:, cols[0]:cols[1]])

    cos = cos_ref[...]
    sin = sin_ref[...]
    mla_scale = (MLA_NOPE + MLA_ROPE) ** -0.5 * LOG2_E
    nope_w = MLA_HEADS * MLA_NOPE

    q = _dot(_rms(proj(_C_Q), qg_ref[...]).astype(BF16), wuq_ref[...])
    q_rope = q[:, nope_w:]
    cos8 = jnp.concatenate([cos] * MLA_HEADS, axis=-1)
    sin8 = jnp.concatenate([sin] * MLA_HEADS, axis=-1)
    q_rope = q_rope * cos8 + _swap_rope_halves(q_rope) * sin8
    for h in range(MLA_HEADS):
        q_out[:, 2 * LANES * h:2 * LANES * h + LANES] = (q[:, LANES * h:LANES * (h + 1)] * mla_scale).astype(BF16)
        q_out[:, 2 * LANES * h + LANES:2 * LANES * (h + 1)] = (
            q_rope[:, LANES * h:LANES * (h + 1)] * mla_scale).astype(BF16)

    kvn = _rms(proj(_C_KV), kvg_ref[...]).astype(BF16)
    k_nope = _dot(kvn, wuk_ref[...])
    k_rope = proj(_C_KR)
    k_rope = (k_rope * cos + _swap_rope_halves(k_rope) * sin).astype(BF16)
    for h in range(MLA_HEADS):
        k_out[:, 2 * LANES * h:2 * LANES * h + LANES] = k_nope[:, LANES * h:LANES * (h + 1)].astype(BF16)
        k_out[:, 2 * LANES * h + LANES:2 * LANES * (h + 1)] = k_rope
    vt_out[...] = _dot_nt(wuvt_ref[...], kvn).astype(BF16)

    u = jax.nn.gelu(proj(_C_GU))
    v = _layer_norm(jax.nn.gelu(proj(_C_GV)), lng_ref[...], lnb_ref[...])
    row = lax.broadcasted_iota(jnp.int32, (GMLP_CHUNK, GMLP_CHUNK), 0)
    col = lax.broadcasted_iota(jnp.int32, (GMLP_CHUNK, GMLP_CHUNK), 1)
    causal = col <= row
    for g in range(GMLP_GROUPS):
        ws = jnp.where(causal, ws_ref[g], 0.0).astype(BF16)
        bias = bst_ref[:, g:g + 1]
        for c in range(tm // GMLP_CHUNK):
            rows = slice(c * GMLP_CHUNK, (c + 1) * GMLP_CHUNK)
            cols = slice(g * LANES, (g + 1) * LANES)
            mixed = _dot(ws, v[rows, cols].astype(BF16)) + bias
            og_out[rows, cols] = (u[rows, cols] * mixed).astype(BF16)

    dq_out[...] = (proj(_C_DQ) * (DIFF_QK ** -0.5 * LOG2_E)).astype(BF16)
    dk_out[...] = proj(_C_DK).astype(BF16)
    dvt_out[...] = _dot_nt(wdvt_ref[...], xb).astype(BF16)


def _proj_in(x2, cos, sin, wa, wdvt, qg, wuq, kvg, wuk, wuvt, lng, lnb, ws, bst, *, seq, tm):
    T, D = x2.shape
    n_pos_blocks = seq // tm
    row = lambda i: (i, 0)
    col = lambda i: (0, i)
    pos = lambda i: (i % n_pos_blocks, 0)
    outs = ((2 * LANES * MLA_HEADS, False), (2 * LANES * MLA_HEADS, False), (MLA_V * MLA_HEADS, True),
            (GMLP_GROUPS * LANES, False), (DIFF_HEADS * LANES, False), (DIFF_HEADS * LANES, False),
            (DIFF_HEADS * DIFF_V, True))
    consts = (wa, wdvt, qg, wuq, kvg, wuk, wuvt, lng, lnb, ws, bst)
    return pl.pallas_call(
        functools.partial(_proj_in_kernel, tm=tm),
        grid=(T // tm,),
        in_specs=[
            pl.BlockSpec((tm, D), row),
            pl.BlockSpec((tm, LANES), pos),
            pl.BlockSpec((tm, LANES), pos),
        ] + [_const_spec(a.shape) for a in consts],
        out_specs=[pl.BlockSpec((w, tm), col) if t else pl.BlockSpec((tm, w), row) for w, t in outs],
        out_shape=[jax.ShapeDtypeStruct((w, T) if t else (T, w), BF16) for w, t in outs],
        compiler_params=_cparams(1),
        name="proj_in",
    )(x2, cos, sin, *consts)


def _causal_pairs(n_blocks):
    qi, ki = [], []
    for q in range(n_blocks):
        for k in range(q + 1):
            qi.append(q)
            ki.append(k)
    return np.asarray(qi, np.int32), np.asarray(ki, np.int32)


def _online_softmax_step(st, vt, m_ref, l_ref, acc_ref):
    m_prev = m_ref[...]
    m_new = jnp.maximum(m_prev, jnp.max(st, axis=0, keepdims=True))
    alpha = jnp.exp2(m_prev - m_new)
    p = jnp.exp2(st - m_new)
    l_ref[...] = alpha * l_ref[...] + jnp.sum(p, axis=0, keepdims=True)
    acc_ref[...] = alpha * acc_ref[...] + _dot(vt, p.astype(BF16))
    m_ref[...] = m_new


def _causal_keep_t(tb):
    key = lax.broadcasted_iota(jnp.int32, (tb, tb), 0)
    query = lax.broadcasted_iota(jnp.int32, (tb, tb), 1)
    return key <= query


def _mla_attn_kernel(qi_ref, ki_ref, q_ref, k_ref, vt_ref, o_ref, m_scr, l_scr, acc_scr, *, tb):
    p = pl.program_id(1)
    qi = qi_ref[p]
    ki = ki_ref[p]

    @pl.when(ki == 0)
    def _init():
        m_scr[...] = jnp.full(m_scr.shape, NEG_INF, F32)
        l_scr[...] = jnp.zeros(l_scr.shape, F32)
        acc_scr[...] = jnp.zeros(acc_scr.shape, F32)

    def step(on_diagonal):
        if on_diagonal:
            keep = _causal_keep_t(tb)
        for h in range(MLA_HEADS):
            qk = slice(2 * LANES * h, 2 * LANES * (h + 1))
            st = _dot_nt(k_ref[:, qk], q_ref[:, qk])
            if on_diagonal:
                st = jnp.where(keep, st, NEG_INF)
            _online_softmax_step(st, vt_ref[MLA_V * h:MLA_V * (h + 1), :], m_scr.at[h], l_scr.at[h], acc_scr.at[h])

    @pl.when(ki != qi)
    def _below():
        step(False)

    @pl.when(ki == qi)
    def _diag():
        step(True)
        for h in range(MLA_HEADS):
            o_ref[:, MLA_V * h:MLA_V * (h + 1)] = (acc_scr[h] / l_scr[h]).T.astype(BF16)


def _mla_attn(q, k, vt, *, batch, seq, tb):
    T = q.shape[0]
    W = vt.shape[0]
    nb = seq // tb
    qi_tab, ki_tab = _causal_pairs(nb)
    qmap = lambda b, p, qi, ki: (b * nb + qi[p], 0)
    kmap = lambda b, p, qi, ki: (b * nb + ki[p], 0)
    vmap = lambda b, p, qi, ki: (0, b * nb + ki[p])
    grid_spec = pltpu.PrefetchScalarGridSpec(
        num_scalar_prefetch=2,
        grid=(batch, len(qi_tab)),
        in_specs=[
            pl.BlockSpec((tb, q.shape[1]), qmap),
            pl.BlockSpec((tb, k.shape[1]), kmap),
            pl.BlockSpec((W, tb), vmap),
        ],
        out_specs=pl.BlockSpec((tb, W), qmap),
        scratch_shapes=[
            pltpu.VMEM((MLA_HEADS, 1, tb), F32),
            pltpu.VMEM((MLA_HEADS, 1, tb), F32),
            pltpu.VMEM((MLA_HEADS, MLA_V, tb), F32),
        ],
    )
    return pl.pallas_call(
        functools.partial(_mla_attn_kernel, tb=tb),
        grid_spec=grid_spec,
        out_shape=jax.ShapeDtypeStruct((T, W), BF16),
        compiler_params=_cparams(2),
        name="mla_attn",
    )(jnp.asarray(qi_tab), jnp.asarray(ki_tab), q, k, vt)


def _t5_causal_bucket(n):
    is_small = n < REL_MAX_EXACT
    nf = jnp.maximum(n, 1).astype(F32)
    large = REL_MAX_EXACT + (jnp.log(nf / REL_MAX_EXACT) / math.log(REL_MAX_DIST / REL_MAX_EXACT)
                             * (REL_BUCKETS - REL_MAX_EXACT)).astype(jnp.int32)
    large = jnp.minimum(large, REL_BUCKETS - 1)
    return jnp.where(is_small, n, large)


def _rel_bias_kernel(tab_ref, o_ref, *, tb):
    slab = pl.program_id(0)
    key = lax.broadcasted_iota(jnp.int32, (tb, tb), 0)
    query = lax.broadcasted_iota(jnp.int32, (tb, tb), 1)
    bucket = _t5_causal_bucket(jnp.maximum(slab * tb + query - key, 0))
    for h in range(DIFF_HEADS):
        far = tab_ref[REL_BUCKETS - 1, h]
        acc = jnp.zeros((tb, tb), F32)
        for b in range(REL_BUCKETS - 1):
            acc = jnp.where(bucket == b, (tab_ref[b, h] - far) * LOG2_E, acc)
        o_ref[h, 0] = acc


def _rel_bias_tiles(rel_bias, *, tb):
    return pl.pallas_call(
        functools.partial(_rel_bias_kernel, tb=tb),
        grid=(2,),
        in_specs=[pl.BlockSpec(memory_space=pltpu.SMEM)],
        out_specs=pl.BlockSpec((DIFF_HEADS, 1, tb, tb), lambda s: (0, s, 0, 0)),
        out_shape=jax.ShapeDtypeStruct((DIFF_HEADS, 2, tb, tb), F32),
        compiler_params=_cparams(1),
        name="rel_bias",
    )(rel_bias)


def _diff_attn_kernel(qi_ref, ki_ref, q_ref, k_ref, vt_ref, bias_ref, lq1_ref, lk1_ref, lq2_ref, lk2_ref, g_ref,
                      o_ref, m_scr, l_scr, acc_scr, *, tb, lambda_init):
    p = pl.program_id(1)
    qi = qi_ref[p]
    ki = ki_ref[p]

    @pl.when(ki == 0)
    def _init():
        m_scr[...] = jnp.full(m_scr.shape, NEG_INF, F32)
        l_scr[...] = jnp.zeros(l_scr.shape, F32)
        acc_scr[...] = jnp.zeros(acc_scr.shape, F32)

    def step(slab):
        if slab == 0:
            keep = _causal_keep_t(tb)
        lane = lax.broadcasted_iota(jnp.int32, (tb, LANES), 1)
        for h in range(DIFF_HEADS):
            hs = slice(LANES * h, LANES * (h + 1))
            qh = q_ref[:, hs]
            kh = k_ref[:, hs]
            vth = vt_ref[DIFF_V * h:DIFF_V * (h + 1), :]
            for m in range(2):
                in_map = (lane < DIFF_QK) if m == 0 else (lane >= DIFF_QK)
                st = _dot_nt(kh, jnp.where(in_map, qh, jnp.zeros_like(qh)))
                if slab is not None:
                    st = st + bias_ref[h, slab]
                if slab == 0:
                    st = jnp.where(keep, st, NEG_INF)
                i = 2 * h + m
                _online_softmax_step(st, vth, m_scr.at[i], l_scr.at[i], acc_scr.at[i])

    @pl.when(ki < qi - 1)
    def _far():
        step(None)

    @pl.when(ki == qi - 1)
    def _near():
        step(1)

    @pl.when(ki == qi)
    def _diag():
        step(0)
        lam = (jnp.exp(jnp.sum(lq1_ref[...] * lk1_ref[...], axis=-1, keepdims=True))
               - jnp.exp(jnp.sum(lq2_ref[...] * lk2_ref[...], axis=-1, keepdims=True)) + lambda_init)
        for h in range(DIFF_HEADS):
            ot = acc_scr[2 * h] / l_scr[2 * h] - lam * (acc_scr[2 * h + 1] / l_scr[2 * h + 1])
            ms = jnp.mean(jnp.square(ot), axis=0, keepdims=True)
            ot = ot * lax.rsqrt(ms + 1e-6) * g_ref[...] * (1.0 - lambda_init)
            o_ref[:, DIFF_V * h:DIFF_V * (h + 1)] = ot.T.astype(BF16)


def _diff_attn(q, k, vt, bias_tiles, lq1, lk1, lq2, lk2, subln_col, *, batch, seq, tb, lambda_init):
    T, W = q.shape
    nb = seq // tb
    qi_tab, ki_tab = _causal_pairs(nb)
    qmap = lambda b, p, qi, ki: (b * nb + qi[p], 0)
    kmap = lambda b, p, qi, ki: (b * nb + ki[p], 0)
    vmap = lambda b, p, qi, ki: (0, b * nb + ki[p])
    small = lambda a: _const_spec(a.shape)
    grid_spec = pltpu.PrefetchScalarGridSpec(
        num_scalar_prefetch=2,
        grid=(batch, len(qi_tab)),
        in_specs=[
            pl.BlockSpec((tb, W), qmap),
            pl.BlockSpec((tb, W), kmap),
            pl.BlockSpec((vt.shape[0], tb), vmap),
            small(bias_tiles), small(lq1), small(lk1), small(lq2), small(lk2), small(subln_col),
        ],
        out_specs=pl.BlockSpec((tb, vt.shape[0]), qmap),
        scratch_shapes=[
            pltpu.VMEM((2 * DIFF_HEADS, 1, tb), F32),
            pltpu.VMEM((2 * DIFF_HEADS, 1, tb), F32),
            pltpu.VMEM((2 * DIFF_HEADS, DIFF_V, tb), F32),
        ],
    )
    return pl.pallas_call(
        functools.partial(_diff_attn_kernel, tb=tb, lambda_init=lambda_init),
        grid_spec=grid_spec,
        out_shape=jax.ShapeDtypeStruct((T, vt.shape[0]), BF16),
        compiler_params=_cparams(2),
        name="diff_attn",
    )(jnp.asarray(qi_tab), jnp.asarray(ki_tab), q, k, vt, bias_tiles, lq1, lk1, lq2, lk2, subln_col)


ROW_SUB = 8
ROW_LANES = 128
ROW_DTYPE = jnp.int32
_HIGH_HALF = -65536


def _split_bf16(x):
    hi = x.astype(BF16)
    lo = (x - hi.astype(F32)).astype(BF16)
    return hi, lo


def _f32_bits(x):
    return lax.bitcast_convert_type(x, jnp.int32)


def _store_row_slabs(dst_ref, val, tm):
    half = ROW_SUB * ROW_LANES
    for c in range(ROW_SUB):
        lo = _f32_bits(val[:, ROW_LANES * c:ROW_LANES * (c + 1)].astype(BF16).astype(F32))
        hi = _f32_bits(val[:, half + ROW_LANES * c:half + ROW_LANES * (c + 1)].astype(BF16).astype(F32))
        word = lax.shift_right_logical(lo, jnp.full_like(lo, 16)) | (hi & _HIGH_HALF)
        dst_ref[pl.ds(c, tm, stride=ROW_SUB), :] = word


def _load_row_slabs(src_ref, tm):
    lows, highs = [], []
    for c in range(ROW_SUB):
        word = src_ref[pl.ds(c, tm, stride=ROW_SUB), :]
        lows.append(lax.bitcast_convert_type(word << 16, F32))
        highs.append(lax.bitcast_convert_type(word & _HIGH_HALF, F32))
    return jnp.concatenate(lows + highs, axis=-1)


def _proj_out_kernel(om_ref, og_ref, od_ref, x_ref, wo_ref, g_ref, b_ref, rwh_ref, rwl_ref,
                     x1_out, x1s_out, logit_out, *, tm, alpha):
    n_m = om_ref.shape[1]
    n_g = og_ref.shape[1]
    mix = (_dot(om_ref[...], wo_ref[0:n_m, :])
           + _dot(og_ref[...], wo_ref[n_m:n_m + n_g, :])
           + _dot(od_ref[...], wo_ref[n_m + n_g:, :]))
    x1 = _layer_norm(alpha * x_ref[...] + mix, g_ref[...], b_ref[...])
    x1_out[...] = x1
    _store_row_slabs(x1s_out, x1, tm)
    xh, xl = _split_bf16(x1)
    logit_out[...] = _dot_nt(rwh_ref[...], xh) + _dot_nt(rwh_ref[...], xl) + _dot_nt(rwl_ref[...], xh)


def _proj_out(om, og, od, x2, wo, g, b, rwh, rwl, *, tm, alpha):
    T, D = x2.shape
    E = rwh.shape[0]
    row = lambda i: (i, 0)
    return pl.pallas_call(
        functools.partial(_proj_out_kernel, tm=tm, alpha=alpha),
        grid=(T // tm,),
        in_specs=[
            pl.BlockSpec((tm, om.shape[1]), row), pl.BlockSpec((tm, og.shape[1]), row),
            pl.BlockSpec((tm, od.shape[1]), row), pl.BlockSpec((tm, D), row),
            _const_spec(wo.shape), _const_spec(g.shape), _const_spec(b.shape),
            _const_spec(rwh.shape), _const_spec(rwl.shape),
        ],
        out_specs=[
            pl.BlockSpec((tm, D), row),
            pl.BlockSpec((tm * ROW_SUB, ROW_LANES), row),
            pl.BlockSpec((E, tm), lambda i: (0, i)),
        ],
        out_shape=[
            jax.ShapeDtypeStruct((T, D), F32),
            jax.ShapeDtypeStruct((T * ROW_SUB, ROW_LANES), ROW_DTYPE),
            jax.ShapeDtypeStruct((E, T), F32),
        ],
        compiler_params=_cparams(1),
        name="proj_out",
    )(om, og, od, x2, wo, g, b, rwh, rwl)


def _first_argmax(v, axis, size):
    mx = jnp.max(v, axis=axis, keepdims=True)
    idx = lax.broadcasted_iota(jnp.int32, v.shape, axis)
    first = jnp.min(jnp.where(v == mx, idx, size), axis=axis, keepdims=True)
    return mx, first


def _route_kernel(logit_ref, bias_ref, gate_out, sel_out):
    E, tn = logit_ref.shape
    per_group = E // N_GROUPS
    scores = jax.nn.sigmoid(logit_ref[...])
    biased = scores + bias_ref[...]
    grouped = biased.reshape(N_GROUPS, per_group, tn)
    member = lax.broadcasted_iota(jnp.int32, grouped.shape, 1)
    top1, first = _first_argmax(grouped, 1, per_group)
    top2 = jnp.max(jnp.where(member == first, NEG_INF, grouped), axis=1, keepdims=True)
    group_score = (top1 + top2).reshape(N_GROUPS, tn)

    group_id = lax.broadcasted_iota(jnp.int32, group_score.shape, 0)
    group_sel = jnp.zeros(group_score.shape, F32)
    remaining = group_score
    for _ in range(TOPK_GROUPS):
        _, first = _first_argmax(remaining, 0, N_GROUPS)
        hit = group_id == first
        group_sel = jnp.where(hit, 1.0, group_sel)
        remaining = jnp.where(hit, NEG_INF, remaining)

    masked = jnp.where(group_sel.reshape(N_GROUPS, 1, tn) > 0.5, grouped, NEG_INF).reshape(E, tn)
    expert_id = lax.broadcasted_iota(jnp.int32, masked.shape, 0)
    sel = jnp.zeros(masked.shape, F32)
    for _ in range(TOP_K):
        _, first = _first_argmax(masked, 0, E)
        hit = expert_id == first
        sel = jnp.where(hit, 1.0, sel)
        masked = jnp.where(hit, NEG_INF, masked)

    w = scores * sel
    gate_out[...] = w / jnp.sum(w, axis=0, keepdims=True) * ROUTED_SCALE
    sel_out[...] = sel


def _route(logits_t, bias_col, *, tn):
    E, T = logits_t.shape
    col = lambda i: (0, i)
    return pl.pallas_call(
        _route_kernel,
        grid=(T // tn,),
        in_specs=[pl.BlockSpec((E, tn), col), _const_spec(bias_col.shape)],
        out_specs=[pl.BlockSpec((E, tn), col), pl.BlockSpec((E, tn), col)],
        out_shape=[jax.ShapeDtypeStruct((E, T), F32), jax.ShapeDtypeStruct((E, T), F32)],
        compiler_params=_cparams(1),
        name="route",
    )(logits_t, bias_col)


def _plan_kernel(sel_ref, gate_ref, start_ref, pos_out, gate_out, carry):
    E, tn = sel_ref.shape

    @pl.when(pl.program_id(0) == 0)
    def _init():
        carry[...] = jnp.zeros(carry.shape, F32)

    sel = sel_ref[...]
    selb = sel.astype(BF16)
    earlier_token = (lax.broadcasted_iota(jnp.int32, (tn, tn), 0) < lax.broadcasted_iota(jnp.int32, (tn, tn), 1))
    rank = _dot(selb, jnp.where(earlier_token, 1.0, 0.0).astype(BF16)) + carry[...]
    smaller_expert = (lax.broadcasted_iota(jnp.int32, (E, E), 1) < lax.broadcasted_iota(jnp.int32, (E, E), 0))
    slot = _dot(jnp.where(smaller_expert, 1.0, 0.0).astype(BF16), selb)
    pos = start_ref[...] + rank
    gates = gate_ref[...]
    for k in range(TOP_K):
        hit = jnp.logical_and(sel > 0.5, slot == float(k))
        pos_out[k:k + 1, :] = jnp.sum(jnp.where(hit, pos, 0.0), axis=0, keepdims=True).astype(jnp.int32)
        gate_out[k:k + 1, :] = jnp.sum(jnp.where(hit, gates, 0.0), axis=0, keepdims=True)
    carry[...] = carry[...] + jnp.sum(sel, axis=1, keepdims=True)


def _plan(sel_t, gates_t, group_start_col, *, tn):
    E, T = sel_t.shape
    col = lambda i: (0, i)
    return pl.pallas_call(
        _plan_kernel,
        grid=(T // tn,),
        in_specs=[pl.BlockSpec((E, tn), col), pl.BlockSpec((E, tn), col), _const_spec(group_start_col.shape)],
        out_specs=[pl.BlockSpec((TOP_K, tn), col), pl.BlockSpec((TOP_K, tn), col)],
        out_shape=[jax.ShapeDtypeStruct((TOP_K, T), jnp.int32), jax.ShapeDtypeStruct((TOP_K, T), F32)],
        scratch_shapes=[pltpu.VMEM((E, 1), F32)],
        compiler_params=_cparams(1),
        name="plan",
    )(sel_t, gates_t, group_start_col)


def _group_layout(sel_t, *, tm, n_tiles):
    E = sel_t.shape[0]
    counts = jnp.sum(sel_t, axis=1).astype(jnp.int32)
    padded = (counts + tm - 1) // tm * tm
    group_end = jnp.cumsum(padded)
    group_start = group_end - padded
    tile_start = jnp.arange(n_tiles, dtype=jnp.int32) * tm
    tile_expert = jnp.sum((tile_start[:, None] >= group_end[None, :]).astype(jnp.int32), axis=1)
    tile_valid = tile_expert < E
    n_valid = jnp.sum(tile_valid.astype(jnp.int32))
    last_expert = jnp.max(jnp.where(counts > 0, jnp.arange(E, dtype=jnp.int32), 0))
    tile_expert = jnp.where(tile_valid, tile_expert, last_expert).astype(jnp.int32)
    tile_block = jnp.minimum(jnp.arange(n_tiles, dtype=jnp.int32), jnp.maximum(n_valid - 1, 0))
    pad_start = (group_start + counts).astype(jnp.int32)
    pad_len = (padded - counts).astype(jnp.int32)
    ids = jnp.arange(E, dtype=jnp.int32)
    later_nonempty = jnp.logical_and(ids[None, :] > ids[:, None], counts[None, :] > 0)
    next_expert = jnp.min(jnp.where(later_nonempty, ids[None, :], E), axis=1)
    next_expert = jnp.where(next_expert == E, -1, next_expert).astype(jnp.int32)
    buffer_of = ((jnp.cumsum((counts > 0).astype(jnp.int32)) - 1) % 2).astype(jnp.int32)
    return (group_start, pad_start, pad_len, n_valid.reshape(1).astype(jnp.int32),
            tile_expert, tile_valid.astype(jnp.int32), tile_block, next_expert[tile_expert], buffer_of[tile_expert])


def _dispatch_kernel(pad_ref, padlen_ref, nvalid_ref, pos_ref, xs_ref, x_ref, sg_ref, su_ref, sd_ref,
                     xs_hbm, shared_out, zbuf, sems, *, td, tm, n_experts, n_tiles):
    i = pl.program_id(0)

    def zero_fill(wait):
        def piece(row, n_rows):
            copy = pltpu.make_async_copy(
                zbuf.at[pl.ds(0, n_rows * ROW_SUB)],
                xs_hbm.at[pl.ds(pl.multiple_of(row * ROW_SUB, ROW_SUB), n_rows * ROW_SUB)], sems.at[0])
            if wait:
                copy.wait()
            else:
                copy.start()

        def expert_padding(e, carry):
            row = pad_ref[e]
            n = padlen_ref[e]
            size = tm // 2
            while size >= 1:
                pl.when((n & size) != 0)(functools.partial(piece, row, size))
                row = row + (n & size)
                size //= 2
            return carry
        lax.fori_loop(0, n_experts, expert_padding, 0)

        def unused_tile(j, carry):
            piece(j * tm, tm)
            return carry
        lax.fori_loop(nvalid_ref[0], n_tiles, unused_tile, 0)

    @pl.when(i == 0)
    def _start_zero_fill():
        zbuf[...] = jnp.zeros(zbuf.shape, ROW_DTYPE)
        zero_fill(wait=False)

    for t in range(td):
        for k in range(TOP_K):
            pltpu.make_async_copy(
                xs_ref.at[pl.ds(t * ROW_SUB, ROW_SUB)],
                xs_hbm.at[pl.ds(pl.multiple_of(pos_ref[k, t] * ROW_SUB, ROW_SUB), ROW_SUB)],
                sems.at[1]).start(priority=k % 2)

    xb = x_ref[...].astype(BF16)
    hs = (jax.nn.silu(_dot(xb, sg_ref[...])) * _dot(xb, su_ref[...])).astype(BF16)
    shared_out[...] = _dot(hs, sd_ref[...])

    for k in range(TOP_K):
        pltpu.make_async_copy(xs_ref, xs_hbm.at[pl.ds(0, td * ROW_SUB)], sems.at[1]).wait()

    @pl.when(i == 0)
    def _finish_zero_fill():
        zero_fill(wait=True)


def _dispatch(pad_start, pad_len, n_valid, pos8, x1s, x1, sg, su, sd, *, td, tm, n_tiles):
    T, D = x1.shape
    E = pad_start.shape[0]
    row = lambda i, *_: (i, 0)
    grid_spec = pltpu.PrefetchScalarGridSpec(
        num_scalar_prefetch=3,
        grid=(T // td,),
        in_specs=[
            pl.BlockSpec((TOP_K, td), lambda i, *_: (0, i), memory_space=pltpu.SMEM),
            pl.BlockSpec((td * ROW_SUB, ROW_LANES), row),
            pl.BlockSpec((td, D), row),
            _const_spec(sg.shape), _const_spec(su.shape), _const_spec(sd.shape),
        ],
        out_specs=[pl.BlockSpec(memory_space=pl.ANY), pl.BlockSpec((td, D), row)],
        scratch_shapes=[pltpu.VMEM((tm * ROW_SUB, ROW_LANES), ROW_DTYPE), pltpu.SemaphoreType.DMA((2,))],
    )
    return pl.pallas_call(
        functools.partial(_dispatch_kernel, td=td, tm=tm, n_experts=E, n_tiles=n_tiles),
        grid_spec=grid_spec,
        out_shape=[jax.ShapeDtypeStruct((n_tiles * tm * ROW_SUB, ROW_LANES), ROW_DTYPE),
                   jax.ShapeDtypeStruct((T, D), F32)],
        compiler_params=_cparams(1),
        name="dispatch",
    )(pad_start, pad_len, n_valid, pos8, x1s, x1, sg, su, sd)


def _experts_kernel(te_ref, tv_ref, tb_ref, nx_ref, buf_ref, xs_ref, wg_hbm, wu_hbm, wd_hbm, ys_ref,
                    wg32, wu32, wd32, wgb, wub, wdb, sems, *, tm, layer):
    i = pl.program_id(0)

    def weight_copies(expert, slot):
        return [pltpu.make_async_copy(hbm.at[layer, expert], buf.at[slot], sems.at[slot])
                for hbm, buf in ((wg_hbm, wg32), (wu_hbm, wu32), (wd_hbm, wd32))]

    @pl.when(i == 0)
    def _first_expert():
        for copy in weight_copies(te_ref[0], buf_ref[0]):
            copy.start()

    @pl.when(tv_ref[i] > 0)
    def _tile():
        @pl.when(jnp.logical_or(i == 0, te_ref[i] != te_ref[jnp.maximum(i - 1, 0)]))
        def _new_expert():
            slot = buf_ref[i]
            for copy in weight_copies(te_ref[i], slot):
                copy.wait()

            @pl.when(nx_ref[i] >= 0)
            def _prefetch_next():
                for copy in weight_copies(nx_ref[i], 1 - slot):
                    copy.start()

            wgb[...] = wg32[slot].astype(BF16)
            wub[...] = wu32[slot].astype(BF16)
            wdb[...] = wd32[slot].astype(BF16)

        xb = _load_row_slabs(xs_ref, tm).astype(BF16)
        h = (jax.nn.silu(_dot(xb, wgb[...])) * _dot(xb, wub[...])).astype(BF16)
        _store_row_slabs(ys_ref, _dot(h, wdb[...]), tm)

    @pl.when(tv_ref[i] == 0)
    def _unused_tile():
        ys_ref[...] = jnp.zeros(ys_ref.shape, ROW_DTYPE)


def _experts(tile_expert, tile_valid, tile_block, tile_next, tile_buffer, xs, wg, wu, wd, layer, *, tm):
    n_tiles = tile_expert.shape[0]
    D, F = wg.shape[2], wg.shape[3]
    grid_spec = pltpu.PrefetchScalarGridSpec(
        num_scalar_prefetch=5,
        grid=(n_tiles,),
        in_specs=[
            pl.BlockSpec((tm * ROW_SUB, ROW_LANES), lambda i, te, tv, tb, nx, bf: (tb[i], 0)),
            pl.BlockSpec(memory_space=pl.ANY),
            pl.BlockSpec(memory_space=pl.ANY),
            pl.BlockSpec(memory_space=pl.ANY),
        ],
        out_specs=pl.BlockSpec((tm * ROW_SUB, ROW_LANES), lambda i, te, tv, tb, nx, bf: (i, 0)),
        scratch_shapes=[
            pltpu.VMEM((2, D, F), F32), pltpu.VMEM((2, D, F), F32), pltpu.VMEM((2, F, D), F32),
            pltpu.VMEM((D, F), BF16), pltpu.VMEM((D, F), BF16), pltpu.VMEM((F, D), BF16),
            pltpu.SemaphoreType.DMA((2,)),
        ],
    )
    return pl.pallas_call(
        functools.partial(_experts_kernel, tm=tm, layer=layer),
        grid_spec=grid_spec,
        out_shape=jax.ShapeDtypeStruct(xs.shape, ROW_DTYPE),
        compiler_params=_cparams(1),
        name="experts",
    )(tile_expert, tile_valid, tile_block, tile_next, tile_buffer, xs, wg, wu, wd)


def _ffn_out_kernel(pos_ref, next_pos_ref, x_ref, shared_ref, gate_ref, ys_hbm, g_ref, b_ref, o_ref,
                    gbuf, sems, *, tc, alpha):
    i = pl.program_id(0)
    n_steps = pl.num_programs(0)
    slab = tc * ROW_SUB

    def start_gather(table_ref, slot):
        for t in range(tc):
            for k in range(TOP_K):
                pltpu.make_async_copy(
                    ys_hbm.at[pl.ds(pl.multiple_of(table_ref[k, t] * ROW_SUB, ROW_SUB), ROW_SUB)],
                    gbuf.at[slot, k, pl.ds(t * ROW_SUB, ROW_SUB)],
                    sems.at[slot]).start(priority=k % 2)

    slot = i % 2

    @pl.when(i == 0)
    def _first_tile():
        start_gather(pos_ref, 0)

    @pl.when(i + 1 < n_steps)
    def _next_tile():
        start_gather(next_pos_ref, 1 - slot)

    x1 = x_ref[...]
    ffn = shared_ref[...]
    gate = gate_ref[...]
    for k in range(TOP_K):
        pltpu.make_async_copy(ys_hbm.at[pl.ds(0, slab)], gbuf.at[slot, k], sems.at[slot]).wait()
    for k in range(TOP_K):
        ffn = ffn + gate[:, k:k + 1] * _load_row_slabs(gbuf.at[slot, k], tc)
    o_ref[...] = _layer_norm(alpha * x1 + ffn, g_ref[...], b_ref[...])


def _ffn_out(pos8, x1, shared, gate_tk, ys, g, b, *, tc, alpha):
    T, D = x1.shape
    n_steps = T // tc
    row = lambda i: (i, 0)
    return pl.pallas_call(
        functools.partial(_ffn_out_kernel, tc=tc, alpha=alpha),
        grid=(n_steps,),
        in_specs=[
            pl.BlockSpec((TOP_K, tc), lambda i: (0, i), memory_space=pltpu.SMEM),
            pl.BlockSpec((TOP_K, tc), lambda i: (0, jnp.minimum(i + 1, n_steps - 1)), memory_space=pltpu.SMEM),
            pl.BlockSpec((tc, D), row),
            pl.BlockSpec((tc, D), row),
            pl.BlockSpec((tc, TOP_K), row),
            pl.BlockSpec(memory_space=pl.ANY),
            _const_spec(g.shape), _const_spec(b.shape),
        ],
        out_specs=pl.BlockSpec((tc, D), row),
        out_shape=jax.ShapeDtypeStruct((T, D), F32),
        scratch_shapes=[pltpu.VMEM((2, TOP_K, tc * ROW_SUB, ROW_LANES), ROW_DTYPE), pltpu.SemaphoreType.DMA((2,))],
        compiler_params=_cparams(1),
        name="ffn_out",
    )(pos8, pos8, x1, shared, gate_tk, ys, g, b)


def _tile_sizes(batch, seq):
    T = batch * seq
    return dict(
        tm=min(256, seq),
        tb=min(512, seq),
        tn=min(512, T),
        td=min(256, T),
        te=min(256, T),
        tc=min(128, T),
    )


def _rope_tables(seq):
    half = MLA_ROPE // 2
    inv = ROPE_THETA ** (-jnp.arange(half, dtype=F32) / half)
    ang = jnp.arange(seq, dtype=jnp.int32).astype(F32)[:, None] * inv[None, :]
    cos, sin = jnp.cos(ang), jnp.sin(ang)
    pad = jnp.zeros((seq, LANES - MLA_ROPE), F32)
    return jnp.concatenate([cos, cos, pad], axis=1), jnp.concatenate([-sin, sin, pad], axis=1)


def _pack_in_weight(w):
    D = w.shape[0]
    kr_end = 512 + 256 + MLA_ROPE
    dv_start = w.shape[1] - DIFF_HEADS * DIFF_V
    wa = jnp.concatenate([w[:, :kr_end], jnp.zeros((D, LANES - MLA_ROPE), w.dtype), w[:, kr_end:dv_start]], axis=1)
    return wa.astype(BF16), w[:, dv_start:].T.astype(BF16)


def _pack_uq(w):
    r = w.shape[0]
    w3 = w.reshape(r, MLA_HEADS, MLA_NOPE + MLA_ROPE)
    nope = w3[:, :, :MLA_NOPE].reshape(r, MLA_HEADS * MLA_NOPE)
    rope = jnp.pad(w3[:, :, MLA_NOPE:], ((0, 0), (0, 0), (0, LANES - MLA_ROPE))).reshape(r, MLA_HEADS * LANES)
    return jnp.concatenate([nope, rope], axis=1).astype(BF16)


def _pack_ukv(w):
    r = w.shape[0]
    w3 = w.reshape(r, MLA_HEADS, MLA_NOPE + MLA_V)
    return w3[:, :, :MLA_NOPE].reshape(r, -1).astype(BF16), w3[:, :, MLA_NOPE:].reshape(r, -1).T.astype(BF16)


def kernel(x, rel_bias, w_in, mla_q_norm, mla_w_uq, mla_kv_norm, mla_w_ukv, gmlp_ln_g, gmlp_ln_b, gmlp_w_s, gmlp_b_s, diff_lq1, diff_lk1, diff_lq2, diff_lk2, diff_subln, w_o, ln1_g, ln1_b, router_w, router_bias, exp_w_gate, exp_w_up, exp_w_down, shared_w_gate, shared_w_up, shared_w_down, ln2_g, ln2_b):
    B, S, D = x.shape
    T = B * S
    depth = w_in.shape[0]
    E = router_w.shape[-1]
    ts = _tile_sizes(B, S)
    alpha = (2 * depth) ** 0.25
    te = ts["te"]
    n_tiles = (T * TOP_K) // te + E

    cos, sin = _rope_tables(S)
    bias_tiles = _rel_bias_tiles(rel_bias, tb=ts["tb"])
    row2 = lambda a: a.reshape(1, -1)

    x2 = x.reshape(T, D)
    for l in range(depth):
        lambda_init = 0.8 - 0.6 * math.exp(-0.3 * l)
        wa, wdvt = _pack_in_weight(w_in[l])
        wuk, wuvt = _pack_ukv(mla_w_ukv[l])
        q, k, vt, og, dq, dk, dvt = _proj_in(
            x2, cos, sin, wa, wdvt, row2(mla_q_norm[l]), _pack_uq(mla_w_uq[l]),
            row2(mla_kv_norm[l]), wuk, wuvt, row2(gmlp_ln_g[l]), row2(gmlp_ln_b[l]),
            gmlp_w_s[l], gmlp_b_s[l].T, seq=S, tm=ts["tm"])
        om = _mla_attn(q, k, vt, batch=B, seq=S, tb=ts["tb"])
        od = _diff_attn(dq, dk, dvt, bias_tiles, row2(diff_lq1[l]), row2(diff_lk1[l]), row2(diff_lq2[l]),
                        row2(diff_lk2[l]), diff_subln[l].reshape(-1, 1), batch=B, seq=S, tb=ts["tb"],
                        lambda_init=lambda_init)
        rwh, rwl = _split_bf16(router_w[l].T)
        x1, x1s, logits_t = _proj_out(om, og, od, x2, w_o[l].astype(BF16), row2(ln1_g[l]), row2(ln1_b[l]),
                                      rwh, rwl, tm=ts["tm"], alpha=alpha)
        gates_t, sel_t = _route(logits_t, router_bias[l].reshape(E, 1), tn=ts["tn"])
        (group_start, pad_start, pad_len, n_valid, tile_expert, tile_valid, tile_block, tile_next,
         tile_buffer) = _group_layout(sel_t, tm=te, n_tiles=n_tiles)
        pos8, gate8 = _plan(sel_t, gates_t, group_start.astype(F32).reshape(E, 1), tn=ts["tn"])
        xs, shared = _dispatch(pad_start, pad_len, n_valid, pos8, x1s, x1, shared_w_gate[l].astype(BF16),
                               shared_w_up[l].astype(BF16), shared_w_down[l].astype(BF16),
                               td=ts["td"], tm=te, n_tiles=n_tiles)
        ys = _experts(tile_expert, tile_valid, tile_block, tile_next, tile_buffer, xs,
                      exp_w_gate, exp_w_up, exp_w_down, l, tm=te)
        x2 = _ffn_out(pos8, x1, shared, gate8.T, ys, row2(ln2_g[l]), row2(ln2_b[l]), tc=ts["tc"], alpha=alpha)
    return x2.reshape(B, S, D)
```
